```python
import math
import jax, jax.numpy as jnp
from jax import lax
import numpy as np

D_MODEL = 1024
BATCH = 8
SEQ = 2048
DEPTH = 1

EXPAND = 2
D_MIX = EXPAND * D_MODEL
GROUP_W = D_MIX // 2
RET_HEADS = 8
RET_HEAD_DIM = GROUP_W // RET_HEADS
CHUNK = 128
ROPE_THETA = 10000.0
POOL_WINDOWS = (2, 4, 8, 16)
N_POOL = len(POOL_WINDOWS)
POOL_C = GROUP_W // N_POOL
N_PROJ = 6
EPS = 1e-6

kernel_name = "hybrid_retention_pool_parallel_heads"


def rms_norm(x, g):
    xf = x.astype(jnp.float32)
    y = xf * lax.rsqrt(jnp.mean(xf * xf, axis=-1, keepdims=True) + EPS)
    return (y * g.astype(jnp.float32)).astype(x.dtype)


def apply_rope(t, positions):
    half = t.shape[-1] // 2
    inv_freq = ROPE_THETA ** (-jnp.arange(half, dtype=jnp.float32) / half)
    ang = positions.astype(jnp.float32)[:, :, None] * inv_freq[None, None, :]
    cos = jnp.cos(ang)[:, :, None, :]
    sin = jnp.sin(ang)[:, :, None, :]
    tf = t.astype(jnp.float32)
    t1, t2 = tf[..., :half], tf[..., half:]
    return jnp.concatenate([t1 * cos - t2 * sin, t1 * sin + t2 * cos], axis=-1)


def retention_chunkwise(q, k, v):
    b, s, h, d = q.shape
    n = s // CHUNK
    def to_chunks(t):
        return t.reshape(b, n, CHUNK, h, d).transpose(0, 3, 1, 2, 4)
    qc, kc, vc = to_chunks(q), to_chunks(k), to_chunks(v)
    gammas = 1.0 - 2.0 ** (-5.0 - jnp.arange(h, dtype=jnp.float32))
    log_g = jnp.log(gammas)
    idx = jnp.arange(CHUNK, dtype=jnp.float32)
    rel = idx[:, None] - idx[None, :]
    decay = jnp.where(rel[None] >= 0,
                      jnp.exp(log_g[:, None, None] * jnp.maximum(rel, 0.0)[None]),
                      0.0)
    scores = jnp.einsum('bhncd,bhnsd->bhncs', qc, kc) * decay[None, :, None]
    inner = jnp.einsum('bhncs,bhnse->bhnce', scores, vc)
    zeta = jnp.exp(log_g[:, None] * (CHUNK - 1.0 - idx)[None, :])
    kv = jnp.einsum('bhnsd,hs,bhnse->bhnde', kc, zeta, vc)
    chunk_decay = jnp.exp(log_g * CHUNK)[None, :, None, None]

    def step(state, kv_n):
        return chunk_decay * state + kv_n, state

    init = jnp.zeros((b, h, d, d), jnp.float32)
    _, r_prev = lax.scan(step, init, jnp.moveaxis(kv, 2, 0))
    r_prev = jnp.moveaxis(r_prev, 0, 2)
    xi = jnp.exp(log_g[:, None] * (idx + 1.0)[None, :])
    cross = jnp.einsum('bhncd,bhnde->bhnce', qc, r_prev) * xi[None, :, None, :, None]
    out = inner + cross
    return out.transpose(0, 2, 3, 1, 4).reshape(b, s, h, d)


def retention_branch(q, k, v, positions, norm_g):
    b, s, _ = q.shape
    shp = (b, s, RET_HEADS, RET_HEAD_DIM)
    qh = apply_rope(q.reshape(shp), positions)
    kh = apply_rope(k.reshape(shp), positions) * (RET_HEAD_DIM ** -0.5)
    vh = v.reshape(shp).astype(jnp.float32)
    o = retention_chunkwise(qh, kh, vh)
    mu = jnp.mean(o, axis=-1, keepdims=True)
    var = jnp.mean(jnp.square(o - mu), axis=-1, keepdims=True)
    o = ((o - mu) * lax.rsqrt(var + EPS)).reshape(b, s, GROUP_W)
    return o * norm_g.astype(jnp.float32)


def pooling_branch(u, pool_w, pool_scale):
    b, s, _ = u.shape
    uf = u.astype(jnp.float32)
    cs = jnp.cumsum(uf, axis=1)
    t = jnp.arange(s, dtype=jnp.float32)
    outs = []
    for g, w in enumerate(POOL_WINDOWS):
        sl = slice(g * POOL_C, (g + 1) * POOL_C)
        cg = cs[..., sl]
        shifted = jnp.pad(cg, ((0, 0), (w, 0), (0, 0)))[:, :s]
        count = jnp.minimum(t + 1.0, float(w))[None, :, None]
        mixed = (cg - shifted) / count - uf[..., sl]
        outs.append(jnp.einsum('bsc,cd->bsd', mixed, pool_w[g].astype(jnp.float32)))
    return jnp.concatenate(outs, axis=-1) * pool_scale.astype(jnp.float32)


def hybrid_layer(x, positions, w_in, w_out, pool_w, pool_scale, ret_norm_g, pre_g, post_g):
    h = rms_norm(x, pre_g)
    proj = jnp.einsum('bsd,df->bsf', h, w_in)
    q, k, v, g_ret, u, g_pool = jnp.split(proj, N_PROJ, axis=-1)
    y_ret = retention_branch(q, k, v, positions, ret_norm_g) * jax.nn.silu(g_ret.astype(jnp.float32))
    y_pool = pooling_branch(u, pool_w, pool_scale) * jax.nn.silu(g_pool.astype(jnp.float32))
    y = jnp.concatenate([y_ret, y_pool], axis=-1).astype(x.dtype)
    y = jnp.einsum('bsf,fd->bsd', y, w_out)
    return x + rms_norm(y, post_g)


def setup_inputs(seed: int = 0) -> dict:
    key = jax.random.key(seed)
    ks = jax.random.split(key, 10)
    x = jax.random.normal(ks[0], (BATCH, SEQ, D_MODEL), jnp.float32)
    offs = jax.random.randint(ks[1], (BATCH, 1), 0, 1024, dtype=jnp.int32)
    positions = (offs + jnp.arange(SEQ, dtype=jnp.int32)[None, :]).astype(jnp.int32)
    w_in = jax.random.normal(ks[2], (DEPTH, D_MODEL, N_PROJ * GROUP_W), jnp.float32) * D_MODEL ** -0.5
    w_out = jax.random.normal(ks[3], (DEPTH, D_MIX, D_MODEL), jnp.float32) * D_MIX ** -0.5
    pool_w = jax.random.normal(ks[4], (DEPTH, N_POOL, POOL_C, POOL_C), jnp.float32) * POOL_C ** -0.5
    pool_scale = 1.0 + 0.1 * jax.random.normal(ks[5], (DEPTH, GROUP_W), jnp.float32)
    ret_norm_g = 1.0 + 0.02 * jax.random.normal(ks[6], (DEPTH, GROUP_W), jnp.float32)
    pre_norm_g = 1.0 + 0.02 * jax.random.normal(ks[7], (DEPTH, D_MODEL), jnp.float32)
    post_norm_g = 1.0 + 0.02 * jax.random.normal(ks[8], (DEPTH, D_MODEL), jnp.float32)
    return {"x": x, "positions": positions, "w_in": w_in, "w_out": w_out,
            "pool_w": pool_w, "pool_scale": pool_scale, "ret_norm_g": ret_norm_g,
            "pre_norm_g": pre_norm_g, "post_norm_g": post_norm_g}


def reference(x, positions, w_in, w_out, pool_w, pool_scale, ret_norm_g, pre_norm_g, post_norm_g):
    for layer in range(DEPTH):
        x = hybrid_layer(x, positions, w_in[layer], w_out[layer], pool_w[layer],
                         pool_scale[layer], ret_norm_g[layer], pre_norm_g[layer],
                         post_norm_g[layer])
    return x
```

```python
import functools
import math

import jax
import jax.numpy as jnp
from jax.experimental import pallas as pl
from jax.experimental.pallas import tpu as pltpu

RET_HEADS = 8
HEAD_DIM = 128
ROPE_THETA = 10000.0
POOL_WINDOWS = (2, 4, 8, 16)
N_PROJ = 6
EPS = 1e-6

SEQ_TILE = 256
RET_CHUNK = 128
POOL_HIST = 16
VMEM_LIMIT_BYTES = 56 * 1024 * 1024


def _silu(g):
    return g * (1.0 / (1.0 + jnp.exp(-g)))


def _layer_kernel(x_ref, pos_ref, w_in_ref, w_out_ref, pool_w_ref, pool_scale_ref,
                  ret_g_ref, pre_g_ref, post_g_ref, o_ref,
                  proj_ref, y_ref, state_ref, uhist_ref):
    f32, bf16 = jnp.float32, jnp.bfloat16
    tile = pl.program_id(1)
    n_tok, d_model = x_ref.shape
    group_w = RET_HEADS * HEAD_DIM
    half = HEAD_DIM // 2

    @pl.when(tile == 0)
    def _():
        state_ref[...] = jnp.zeros_like(state_ref)
        uhist_ref[...] = jnp.zeros_like(uhist_ref)

    x = x_ref[...]
    ms = jnp.mean(x * x, axis=-1, keepdims=True)
    hb = (x * jax.lax.rsqrt(ms + EPS) * pre_g_ref[...]).astype(bf16)
    for seg in range(N_PROJ):
        cols = slice(seg * group_w, (seg + 1) * group_w)
        proj_ref[:, cols] = jnp.dot(hb, w_in_ref[:, cols], preferred_element_type=f32)

    lane = jax.lax.broadcasted_iota(jnp.int32, (1, HEAD_DIM), 1)
    freq_idx = (lane & (half - 1)).astype(f32)
    inv_freq = jnp.exp(freq_idx * (-math.log(ROPE_THETA) / half))
    ang = pos_ref[...].astype(f32) * inv_freq
    cos_t = jnp.cos(ang)
    sin_t = jnp.sin(ang)
    sin_t = jnp.where(lane < half, -sin_t, sin_t)

    row = jax.lax.broadcasted_iota(jnp.int32, (RET_CHUNK, HEAD_DIM), 0)
    col = jax.lax.broadcasted_iota(jnp.int32, (RET_CHUNK, HEAD_DIM), 1)
    causal = row >= col
    row_p1 = (row + 1).astype(f32)
    k_scale = HEAD_DIM ** -0.5
    for hd in range(RET_HEADS):
        log_g = math.log(1.0 - 2.0 ** (-5.0 - hd))
        q_decay = jnp.exp(row_p1 * log_g)
        k_decay = jnp.exp(row_p1 * (-log_g)) * k_scale
        chunk_decay = math.exp(log_g * RET_CHUNK)
        hcols = slice(hd * HEAD_DIM, (hd + 1) * HEAD_DIM)
        kcols = slice(group_w + hd * HEAD_DIM, group_w + (hd + 1) * HEAD_DIM)
        vcols = slice(2 * group_w + hd * HEAD_DIM, 2 * group_w + (hd + 1) * HEAD_DIM)
        gcols = slice(3 * group_w + hd * HEAD_DIM, 3 * group_w + (hd + 1) * HEAD_DIM)
        norm_g = ret_g_ref[:, hcols]
        for c in range(n_tok // RET_CHUNK):
            rows = slice(c * RET_CHUNK, (c + 1) * RET_CHUNK)
            cos_c, sin_c = cos_t[rows], sin_t[rows]
            q = proj_ref[rows, hcols]
            k = proj_ref[rows, kcols]
            vb = proj_ref[rows, vcols].astype(bf16)
            qd = ((q * cos_c + pltpu.roll(q, half, axis=1) * sin_c) * q_decay).astype(bf16)
            kd = (k * cos_c + pltpu.roll(k, half, axis=1) * sin_c) * k_decay
            kdt = kd.T.astype(bf16)
            scores = jnp.dot(qd, kdt, preferred_element_type=f32)
            p = jnp.where(causal, scores, 0.0).astype(bf16)
            s_prev = state_ref[hd]
            lhs = jnp.concatenate([p, qd], axis=1)
            rhs = jnp.concatenate([vb, s_prev.astype(bf16)], axis=0)
            o = jnp.dot(lhs, rhs, preferred_element_type=f32)
            kv = jnp.dot(kdt, vb, preferred_element_type=f32)
            state_ref[hd] = (s_prev + kv) * chunk_decay
            mu = jnp.mean(o, axis=-1, keepdims=True)
            dev = o - mu
            var = jnp.mean(dev * dev, axis=-1, keepdims=True)
            gate = _silu(proj_ref[rows, gcols])
            y_ref[rows, hcols] = (dev * jax.lax.rsqrt(var + EPS) * norm_g * gate).astype(bf16)

    u_off, gp_off = 4 * group_w, 5 * group_w
    pool_c = group_w // len(POOL_WINDOWS)
    hist_row = jax.lax.broadcasted_iota(jnp.int32, (POOL_HIST, pool_c), 0)
    t_head = (tile * n_tok + hist_row + 1).astype(f32)
    for g, w in enumerate(POOL_WINDOWS):
        ucols = slice(u_off + g * pool_c, u_off + (g + 1) * pool_c)
        gcols = slice(gp_off + g * pool_c, gp_off + (g + 1) * pool_c)
        pcols = slice(g * pool_c, (g + 1) * pool_c)
        u = proj_ref[:, ucols]
        s = jnp.concatenate([uhist_ref[:, pcols], u], axis=0)
        shift = 1
        while shift < w:
            s = s + pltpu.roll(s, shift, axis=0)
            shift *= 2
        wsum = s[POOL_HIST:]
        inv_head = 1.0 / jnp.minimum(t_head, float(w))
        mean = jnp.concatenate([wsum[:POOL_HIST] * inv_head, wsum[POOL_HIST:] * (1.0 / w)], axis=0)
        mixed = (mean - u).astype(bf16)
        pm = jnp.dot(mixed, pool_w_ref[g], preferred_element_type=f32)
        gate = _silu(proj_ref[:, gcols])
        y_ref[:, group_w + g * pool_c:group_w + (g + 1) * pool_c] = (
            pm * pool_scale_ref[:, pcols] * gate).astype(bf16)
    uhist_ref[...] = proj_ref[n_tok - POOL_HIST:, u_off:u_off + group_w]

    out = jnp.dot(y_ref[...], w_out_ref[...], preferred_element_type=f32)
    ms_o = jnp.mean(out * out, axis=-1, keepdims=True)
    o_ref[...] = x + out * jax.lax.rsqrt(ms_o + EPS) * post_g_ref[...]


def _hybrid_layer(x, pos_col, w_in, w_out, pool_w, pool_scale, ret_g, pre_g, post_g):
    batch, seq, d_model = x.shape
    d_proj = w_in.shape[1]
    d_mix = w_out.shape[0]
    group_w = d_mix // 2
    assert seq % SEQ_TILE == 0 and SEQ_TILE % RET_CHUNK == 0
    assert group_w == RET_HEADS * HEAD_DIM and d_proj == N_PROJ * group_w

    def whole(shape):
        return pl.BlockSpec(shape, lambda b, t: (0,) * len(shape), pipeline_mode=pl.Buffered(1))

    row_spec = pl.BlockSpec((1, group_w), lambda b, t: (0, 0))
    return pl.pallas_call(
        _layer_kernel,
        grid=(batch, seq // SEQ_TILE),
        in_specs=[
            pl.BlockSpec((None, SEQ_TILE, d_model), lambda b, t: (b, t, 0)),
            pl.BlockSpec((None, SEQ_TILE, 1), lambda b, t: (b, t, 0)),
            whole(w_in.shape),
            whole(w_out.shape),
            whole(pool_w.shape),
            row_spec, row_spec,
            pl.BlockSpec((1, d_model), lambda b, t: (0, 0)),
            pl.BlockSpec((1, d_model), lambda b, t: (0, 0)),
        ],
        out_specs=pl.BlockSpec((None, SEQ_TILE, d_model), lambda b, t: (b, t, 0)),
        out_shape=jax.ShapeDtypeStruct(x.shape, x.dtype),
        scratch_shapes=[
            pltpu.VMEM((SEQ_TILE, d_proj), jnp.float32),
            pltpu.VMEM((SEQ_TILE, d_mix), jnp.bfloat16),
            pltpu.VMEM((RET_HEADS, HEAD_DIM, HEAD_DIM), jnp.float32),
            pltpu.VMEM((POOL_HIST, group_w), jnp.float32),
        ],
        compiler_params=pltpu.CompilerParams(
            dimension_semantics=("arbitrary", "arbitrary"),
            vmem_limit_bytes=VMEM_LIMIT_BYTES),
        name="hybrid_layer",
    )(x, pos_col, w_in, w_out, pool_w, pool_scale, ret_g, pre_g, post_g)


@jax.jit
def kernel(x, positions, w_in, w_out, pool_w, pool_scale, ret_norm_g, pre_norm_g, post_norm_g):
    bf16 = jnp.bfloat16
    pos_col = positions[..., None]
    for layer in range(w_in.shape[0]):
        x = _hybrid_layer(
            x, pos_col,
            w_in[layer].astype(bf16), w_out[layer].astype(bf16), pool_w[layer].astype(bf16),
            pool_scale[layer][None, :], ret_norm_g[layer][None, :],
            pre_norm_g[layer][None, :], post_norm_g[layer][None, :])
    return x
```

```python
import functools
import math

import jax
import jax.numpy as jnp
from jax.experimental import pallas as pl
from jax.experimental.pallas import tpu as pltpu

RET_HEADS = 8
HEAD_DIM = 128
ROPE_THETA = 10000.0
POOL_WINDOWS = (2, 4, 8, 16)
N_PROJ = 6
EPS = 1e-6

SEQ_TILE = 512
RET_CHUNK = 128
POOL_HIST = 16
CAST_ROWS = 16
N_DMA_SEMS = 6


def _silu(g):
    return g * (1.0 / (1.0 + jnp.exp(-g)))


def _cast_rows(src_ref, src_row0, src_cols, dst_ref, dst_row0, n_rows):
    def body(i, carry):
        r = pl.multiple_of(i * CAST_ROWS, CAST_ROWS)
        dst_ref[pl.ds(dst_row0 + r, CAST_ROWS), :] = (
            src_ref[pl.ds(src_row0 + r, CAST_ROWS), src_cols].astype(jnp.bfloat16))
        return carry
    jax.lax.fori_loop(0, n_rows // CAST_ROWS, body, 0)


def _stage_weights(w_in_hbm, w_out_hbm, pool_w_hbm, w_in_s, w_out_s, pool_w_s, stage_ref, sems):
    n_tok = stage_ref.shape[0]
    d_model = w_out_hbm.shape[1]
    pool_c = pool_w_hbm.shape[1]
    n_out_chunks = w_out_hbm.shape[0] // n_tok
    pool_rows = pool_w_hbm.shape[0]
    n_pool_chunks = pool_rows // n_tok
    pool_col0 = n_out_chunks * d_model

    def small_copy(i):
        if i < n_out_chunks:
            src = w_out_hbm.at[pl.ds(i * n_tok, n_tok), :]
            dst = stage_ref.at[:, pl.ds(i * d_model, d_model)]
        else:
            j = i - n_out_chunks
            src = pool_w_hbm.at[pl.ds(j * n_tok, n_tok), :]
            dst = stage_ref.at[:, pl.ds(pool_col0 + j * pool_c, pool_c)]
        return pltpu.make_async_copy(src, dst, sems.at[i])

    n_small = n_out_chunks + n_pool_chunks
    for i in range(n_small):
        small_copy(i).start()
    for i in range(n_small):
        small_copy(i).wait()
    for i in range(n_out_chunks):
        _cast_rows(stage_ref, 0, slice(i * d_model, (i + 1) * d_model), w_out_s, i * n_tok, n_tok)
    for j in range(n_pool_chunks):
        _cast_rows(stage_ref, 0, slice(pool_col0 + j * pool_c, pool_col0 + (j + 1) * pool_c),
                   pool_w_s, j * n_tok, n_tok)

    half = n_tok // 2
    n_in_chunks = w_in_hbm.shape[0] // half

    def in_copy(c):
        slot = c % 2
        return pltpu.make_async_copy(w_in_hbm.at[pl.ds(c * half, half), :],
                                     stage_ref.at[pl.ds(slot * half, half), :], sems.at[slot])

    in_copy(0).start()
    in_copy(1).start()
    for c in range(n_in_chunks):
        in_copy(c).wait()
        _cast_rows(stage_ref, (c % 2) * half, slice(None), w_in_s, c * half, half)
        if c + 2 < n_in_chunks:
            in_copy(c + 2).start()


def _layer_kernel(layer, x_ref, pos_ref, w_in_all, w_out_all, pool_w_all, pool_scale_ref,
                  ret_g_ref, pre_g_ref, post_g_ref, o_ref,
                  w_in_s, w_out_s, pool_w_s, proj_ref, y_ref, state_ref, uhist_ref, sems):
    f32, bf16 = jnp.float32, jnp.bfloat16
    w_in_hbm, w_out_hbm, pool_w_hbm = w_in_all.at[layer], w_out_all.at[layer], pool_w_all.at[layer]
    tile = pl.program_id(1)
    n_tok, d_model = x_ref.shape
    group_w = RET_HEADS * HEAD_DIM
    half = HEAD_DIM // 2

    @pl.when((pl.program_id(0) == 0) & (tile == 0))
    def _():
        _stage_weights(w_in_hbm, w_out_hbm, pool_w_hbm, w_in_s, w_out_s, pool_w_s, proj_ref, sems)

    @pl.when(tile == 0)
    def _():
        state_ref[...] = jnp.zeros_like(state_ref)
        uhist_ref[...] = jnp.zeros_like(uhist_ref)

    x = x_ref[...]
    ms = jnp.mean(x * x, axis=-1, keepdims=True)
    hb = (x * jax.lax.rsqrt(ms + EPS) * pre_g_ref[...]).astype(bf16)
    for seg in range(N_PROJ):
        cols = slice(seg * group_w, (seg + 1) * group_w)
        proj_ref[:, cols] = jnp.dot(hb, w_in_s[:, cols], preferred_element_type=f32)

    lane = jax.lax.broadcasted_iota(jnp.int32, (1, HEAD_DIM), 1)
    freq_idx = (lane & (half - 1)).astype(f32)
    inv_freq = jnp.exp(freq_idx * (-math.log(ROPE_THETA) / half))
    ang = pos_ref[...].astype(f32) * inv_freq
    cos_t = jnp.cos(ang)
    sin_t = jnp.sin(ang)
    sin_t = jnp.where(lane < half, -sin_t, sin_t)

    row = jax.lax.broadcasted_iota(jnp.int32, (RET_CHUNK, HEAD_DIM), 0)
    col = jax.lax.broadcasted_iota(jnp.int32, (RET_CHUNK, HEAD_DIM), 1)
    causal = row >= col
    row_p1 = (row + 1).astype(f32)
    k_scale = HEAD_DIM ** -0.5
    for hd in range(RET_HEADS):
        log_g = math.log(1.0 - 2.0 ** (-5.0 - hd))
        q_decay = jnp.exp(row_p1 * log_g)
        k_decay = jnp.exp(row_p1 * (-log_g)) * k_scale
        chunk_decay = math.exp(log_g * RET_CHUNK)
        hcols = slice(hd * HEAD_DIM, (hd + 1) * HEAD_DIM)
        kcols = slice(group_w + hd * HEAD_DIM, group_w + (hd + 1) * HEAD_DIM)
        vcols = slice(2 * group_w + hd * HEAD_DIM, 2 * group_w + (hd + 1) * HEAD_DIM)
        gcols = slice(3 * group_w + hd * HEAD_DIM, 3 * group_w + (hd + 1) * HEAD_DIM)
        norm_g = ret_g_ref[:, hcols]
        for c in range(n_tok // RET_CHUNK):
            rows = slice(c * RET_CHUNK, (c + 1) * RET_CHUNK)
            cos_c, sin_c = cos_t[rows], sin_t[rows]
            q = proj_ref[rows, hcols]
            k = proj_ref[rows, kcols]
            vb = proj_ref[rows, vcols].astype(bf16)
            qd = ((q * cos_c + pltpu.roll(q, half, axis=1) * sin_c) * q_decay).astype(bf16)
            kd = (k * cos_c + pltpu.roll(k, half, axis=1) * sin_c) * k_decay
            kdt = kd.T.astype(bf16)
            scores = jnp.dot(qd, kdt, preferred_element_type=f32)
            p = jnp.where(causal, scores, 0.0).astype(bf16)
            s_prev = state_ref[hd]
            lhs = jnp.concatenate([p, qd], axis=1)
            rhs = jnp.concatenate([vb, s_prev.astype(bf16)], axis=0)
            o = jnp.dot(lhs, rhs, preferred_element_type=f32)
            kv = jnp.dot(kdt, vb, preferred_element_type=f32)
            state_ref[hd] = (s_prev + kv) * chunk_decay
            mu = jnp.mean(o, axis=-1, keepdims=True)
            dev = o - mu
            var = jnp.mean(dev * dev, axis=-1, keepdims=True)
            gate = _silu(proj_ref[rows, gcols])
            y_ref[rows, hcols] = (dev * jax.lax.rsqrt(var + EPS) * norm_g * gate).astype(bf16)

    u_off, gp_off = 4 * group_w, 5 * group_w
    pool_c = group_w // len(POOL_WINDOWS)
    hist_row = jax.lax.broadcasted_iota(jnp.int32, (POOL_HIST, pool_c), 0)
    t_head = (tile * n_tok + hist_row + 1).astype(f32)
    for g, w in enumerate(POOL_WINDOWS):
        ucols = slice(u_off + g * pool_c, u_off + (g + 1) * pool_c)
        gcols = slice(gp_off + g * pool_c, gp_off + (g + 1) * pool_c)
        pcols = slice(g * pool_c, (g + 1) * pool_c)
        u = proj_ref[:, ucols]
        s = jnp.concatenate([uhist_ref[:, pcols], u], axis=0)
        shift = 1
        while shift < w:
            s = s + pltpu.roll(s, shift, axis=0)
            shift *= 2
        wsum = s[POOL_HIST:]
        inv_head = 1.0 / jnp.minimum(t_head, float(w))
        mean = jnp.concatenate([wsum[:POOL_HIST] * inv_head, wsum[POOL_HIST:] * (1.0 / w)], axis=0)
        mixed = (mean - u).astype(bf16)
        pm = jnp.dot(mixed, pool_w_s[g * pool_c:(g + 1) * pool_c, :], preferred_element_type=f32)
        gate = _silu(proj_ref[:, gcols])
        y_ref[:, group_w + g * pool_c:group_w + (g + 1) * pool_c] = (
            pm * pool_scale_ref[:, pcols] * gate).astype(bf16)
    uhist_ref[...] = proj_ref[n_tok - POOL_HIST:, u_off:u_off + group_w]

    out = jnp.dot(y_ref[...], w_out_s[...], preferred_element_type=f32)
    ms_o = jnp.mean(out * out, axis=-1, keepdims=True)
    o_ref[...] = x + out * jax.lax.rsqrt(ms_o + EPS) * post_g_ref[...]


def _vmem_limit_bytes(d_model, d_proj, d_mix, pool_rows, pool_c):
    lane_pad = 128
    scratch = (2 * (d_model * d_proj + d_mix * d_model + pool_rows * pool_c)
               + 4 * SEQ_TILE * d_proj + 2 * SEQ_TILE * d_mix
               + 4 * RET_HEADS * HEAD_DIM * HEAD_DIM + 4 * POOL_HIST * (d_mix // 2))
    blocks = 2 * (2 * 4 * SEQ_TILE * d_model + 4 * SEQ_TILE * lane_pad)
    temporaries = 4 * SEQ_TILE * d_proj
    return scratch + blocks + temporaries


def _hybrid_layer(layer, x, pos_col, w_in, w_out, pool_w, pool_scale, ret_g, pre_g, post_g):
    batch, seq, d_model = x.shape
    d_proj = w_in.shape[2]
    d_mix = w_out.shape[1]
    group_w = d_mix // 2
    _, pool_rows, pool_c = pool_w.shape
    assert seq % SEQ_TILE == 0 and SEQ_TILE % RET_CHUNK == 0 and SEQ_TILE % (2 * CAST_ROWS) == 0
    assert group_w == RET_HEADS * HEAD_DIM and d_proj == N_PROJ * group_w
    assert d_mix % SEQ_TILE == 0 and pool_rows % SEQ_TILE == 0 and d_model % (SEQ_TILE // 2) == 0
    n_small = d_mix // SEQ_TILE + pool_rows // SEQ_TILE
    assert (d_mix // SEQ_TILE) * d_model + (pool_rows // SEQ_TILE) * pool_c <= d_proj
    assert n_small <= N_DMA_SEMS

    hbm = pl.BlockSpec(memory_space=pl.ANY)
    row_spec = pl.BlockSpec((1, group_w), lambda b, t: (0, 0))
    return pl.pallas_call(
        functools.partial(_layer_kernel, layer),
        grid=(batch, seq // SEQ_TILE),
        in_specs=[
            pl.BlockSpec((None, SEQ_TILE, d_model), lambda b, t: (b, t, 0)),
            pl.BlockSpec((None, SEQ_TILE, 1), lambda b, t: (b, t, 0)),
            hbm, hbm, hbm,
            row_spec, row_spec,
            pl.BlockSpec((1, d_model), lambda b, t: (0, 0)),
            pl.BlockSpec((1, d_model), lambda b, t: (0, 0)),
        ],
        out_specs=pl.BlockSpec((None, SEQ_TILE, d_model), lambda b, t: (b, t, 0)),
        out_shape=jax.ShapeDtypeStruct(x.shape, x.dtype),
        scratch_shapes=[
            pltpu.VMEM((d_model, d_proj), jnp.bfloat16),
            pltpu.VMEM((d_mix, d_model), jnp.bfloat16),
            pltpu.VMEM((pool_rows, pool_c), jnp.bfloat16),
            pltpu.VMEM((SEQ_TILE, d_proj), jnp.float32),
            pltpu.VMEM((SEQ_TILE, d_mix), jnp.bfloat16),
            pltpu.VMEM((RET_HEADS, HEAD_DIM, HEAD_DIM), jnp.float32),
            pltpu.VMEM((POOL_HIST, group_w), jnp.float32),
            pltpu.SemaphoreType.DMA((N_DMA_SEMS,)),
        ],
        compiler_params=pltpu.CompilerParams(
            dimension_semantics=("arbitrary", "arbitrary"),
            vmem_limit_bytes=_vmem_limit_bytes(d_model, d_proj, d_mix, pool_rows, pool_c)),
        name="hybrid_layer",
    )(x, pos_col, w_in, w_out, pool_w, pool_scale, ret_g, pre_g, post_g)


@jax.jit
def kernel(x, positions, w_in, w_out, pool_w, pool_scale, ret_norm_g, pre_norm_g, post_norm_g):
    depth = w_in.shape[0]
    pos_col = positions[..., None]
    pool_w2d = pool_w.reshape(depth, -1, pool_w.shape[-1])
    for layer in range(depth):
        x = _hybrid_layer(
            layer, x, pos_col, w_in, w_out, pool_w2d,
            pool_scale[layer][None, :], ret_norm_g[layer][None, :],
            pre_norm_g[layer][None, :], post_norm_g[layer][None, :])
    return x
```

```python
import functools
import math

import jax
import jax.numpy as jnp
from jax.experimental import pallas as pl
from jax.experimental.pallas import tpu as pltpu

RET_HEADS = 8
HEAD_DIM = 128
ROPE_THETA = 10000.0
POOL_WINDOWS = (2, 4, 8, 16)
N_PROJ = 6
EPS = 1e-6

SEQ_TILE = 512
RET_CHUNK = 128
OUT_ROWS = 256
POOL_HIST = 16
CAST_ROWS = 16
LANES = 128
N_DMA_SEMS = 6


def _silu(g):
    return g * (1.0 / (1.0 + jnp.exp(-g)))


def _cast_rows(src_ref, src_row0, src_cols, dst_ref, dst_row0, n_rows):
    def body(i, carry):
        r = pl.multiple_of(i * CAST_ROWS, CAST_ROWS)
        dst_ref[pl.ds(dst_row0 + r, CAST_ROWS), :] = (
            src_ref[pl.ds(src_row0 + r, CAST_ROWS), src_cols].astype(jnp.bfloat16))
        return carry
    jax.lax.fori_loop(0, n_rows // CAST_ROWS, body, 0)


def _stage_weights(w_in_hbm, w_out_hbm, pool_w_hbm, w_in_s, w_out_s, pool_w_s, stage_ref, sems):
    n_tok = stage_ref.shape[0]
    d_model = w_out_hbm.shape[1]
    pool_c = pool_w_hbm.shape[1]
    n_out_chunks = w_out_hbm.shape[0] // n_tok
    pool_rows = pool_w_hbm.shape[0]
    n_pool_chunks = pool_rows // n_tok
    pool_col0 = n_out_chunks * d_model

    def small_copy(i):
        if i < n_out_chunks:
            src = w_out_hbm.at[pl.ds(i * n_tok, n_tok), :]
            dst = stage_ref.at[:, pl.ds(i * d_model, d_model)]
        else:
            j = i - n_out_chunks
            src = pool_w_hbm.at[pl.ds(j * n_tok, n_tok), :]
            dst = stage_ref.at[:, pl.ds(pool_col0 + j * pool_c, pool_c)]
        return pltpu.make_async_copy(src, dst, sems.at[i])

    n_small = n_out_chunks + n_pool_chunks
    for i in range(n_small):
        small_copy(i).start()
    for i in range(n_small):
        small_copy(i).wait()
    for i in range(n_out_chunks):
        _cast_rows(stage_ref, 0, slice(i * d_model, (i + 1) * d_model), w_out_s, i * n_tok, n_tok)
    for j in range(n_pool_chunks):
        _cast_rows(stage_ref, 0, slice(pool_col0 + j * pool_c, pool_col0 + (j + 1) * pool_c),
                   pool_w_s, j * n_tok, n_tok)

    half = n_tok // 2
    n_in_chunks = w_in_hbm.shape[0] // half

    def in_copy(c):
        slot = c % 2
        return pltpu.make_async_copy(w_in_hbm.at[pl.ds(c * half, half), :],
                                     stage_ref.at[pl.ds(slot * half, half), :], sems.at[slot])

    in_copy(0).start()
    in_copy(1).start()
    for c in range(n_in_chunks):
        in_copy(c).wait()
        _cast_rows(stage_ref, (c % 2) * half, slice(None), w_in_s, c * half, half)
        if c + 2 < n_in_chunks:
            in_copy(c + 2).start()


def _lane_inv_freq():
    half = HEAD_DIM // 2
    lane = jax.lax.broadcasted_iota(jnp.int32, (1, HEAD_DIM), 1)
    freq_idx = (lane & (half - 1)).astype(jnp.float32)
    return jnp.exp(freq_idx * (-math.log(ROPE_THETA) / half))


def _offset_tables(cos_off_ref, sin_off_ref):
    n_tok = cos_off_ref.shape[0]
    offset = jax.lax.broadcasted_iota(jnp.int32, (n_tok, HEAD_DIM), 0).astype(jnp.float32)
    ang = offset * _lane_inv_freq()
    cos_off_ref[...] = jnp.cos(ang)
    sin_off_ref[...] = jnp.sin(ang)


def _general_tables(pos_ref, batch_row, cos_ref, sin_ref):
    f32 = jnp.float32
    n_tok = pos_ref.shape[1]
    half = HEAD_DIM // 2
    freq_idx = jax.lax.broadcasted_iota(jnp.int32, (half, LANES), 0).astype(f32)
    inv_freq = jnp.exp(freq_idx * (-math.log(ROPE_THETA) / half))
    inv_freq = jnp.concatenate([inv_freq] * (n_tok // LANES), axis=1)
    pos = pos_ref[pl.ds(batch_row, 1), :].astype(f32)
    ang = inv_freq * pos
    cos_a, sin_a = jnp.cos(ang), jnp.sin(ang)
    cos_ref[...] = jnp.concatenate([cos_a, cos_a], axis=0).T
    sin_ref[...] = jnp.concatenate([-sin_a, sin_a], axis=0).T


def _is_contiguous(pos_ref, batch_row):
    pos = pos_ref[pl.ds(batch_row, 1), :]
    offset = jax.lax.broadcasted_iota(jnp.int32, pos.shape, 1)
    return jnp.min((pos - pos[:, 0:1] == offset).astype(jnp.int32))


def _contiguous_tables(pos_ref, batch_row, cos_off_ref, sin_off_ref):
    half = HEAD_DIM // 2
    lane = jax.lax.broadcasted_iota(jnp.int32, (1, HEAD_DIM), 1)
    first = pos_ref[pl.ds(batch_row, 1), :][:, 0:1].astype(jnp.float32)
    base = first * _lane_inv_freq()
    cos_b, sin_b = jnp.cos(base), jnp.sin(base)
    sign = jnp.where(lane < half, -1.0, 1.0)
    cos_o, sin_o = cos_off_ref[...], sin_off_ref[...]
    cos_t = cos_b * cos_o - sin_b * sin_o
    sin_t = (sign * sin_b) * cos_o + (sign * cos_b) * sin_o
    return cos_t, sin_t


def _layer_kernel(layer, x_ref, pos_ref, posn_ref, w_in_all, w_out_all, pool_w_all,
                  pool_scale_ref, ret_g_ref, pre_g_ref, post_g_ref, o_ref,
                  w_in_s, w_out_s, pool_w_s, proj_ref, cos_off_ref, sin_off_ref,
                  cos_gen_ref, sin_gen_ref, cos_ref, sin_ref, y_ref,
                  state_ref, uhist_ref, contig_ref, sems):
    f32, bf16 = jnp.float32, jnp.bfloat16
    w_in_hbm, w_out_hbm, pool_w_hbm = w_in_all.at[layer], w_out_all.at[layer], pool_w_all.at[layer]
    b, tile = pl.program_id(0), pl.program_id(1)
    n_tiles = pl.num_programs(1)
    n_tok, d_model = x_ref.shape
    group_w = RET_HEADS * HEAD_DIM
    half = HEAD_DIM // 2

    @pl.when((b == 0) & (tile == 0))
    def _():
        _stage_weights(w_in_hbm, w_out_hbm, pool_w_hbm, w_in_s, w_out_s, pool_w_s, proj_ref, sems)
        _offset_tables(cos_off_ref, sin_off_ref)
        cos_gen_ref[...] = jnp.zeros_like(cos_gen_ref)
        sin_gen_ref[...] = jnp.zeros_like(sin_gen_ref)
        contig_ref[0] = _is_contiguous(pos_ref, b)

    @pl.when(tile == 0)
    def _():
        state_ref[...] = jnp.zeros_like(state_ref)
        uhist_ref[...] = jnp.zeros_like(uhist_ref)

    contiguous = contig_ref[0]

    @pl.when(contiguous == 0)
    def _():
        _general_tables(pos_ref, b, cos_gen_ref, sin_gen_ref)

    wraps = tile + 1 == n_tiles
    next_row = jnp.minimum(b + wraps.astype(jnp.int32), pl.num_programs(0) - 1)
    contig_ref[0] = _is_contiguous(posn_ref, next_row)

    x = x_ref[...]
    ms = jnp.mean(x * x, axis=-1, keepdims=True)
    hb = (x * jax.lax.rsqrt(ms + EPS) * pre_g_ref[...]).astype(bf16)
    for seg in range(N_PROJ):
        cols = slice(seg * group_w, (seg + 1) * group_w)
        proj_ref[:, cols] = jnp.dot(hb, w_in_s[:, cols], preferred_element_type=f32)

    use_fast = jnp.broadcast_to(contiguous, (n_tok, HEAD_DIM)) != 0
    cos_fast, sin_fast = _contiguous_tables(pos_ref, b, cos_off_ref, sin_off_ref)
    cos_ref[...] = jnp.where(use_fast, cos_fast, cos_gen_ref[...])
    sin_ref[...] = jnp.where(use_fast, sin_fast, sin_gen_ref[...])

    row = jax.lax.broadcasted_iota(jnp.int32, (RET_CHUNK, HEAD_DIM), 0)
    col = jax.lax.broadcasted_iota(jnp.int32, (RET_CHUNK, HEAD_DIM), 1)
    causal = row >= col
    row_p1 = (row + 1).astype(f32)
    k_scale = HEAD_DIM ** -0.5
    for hd in range(RET_HEADS):
        log_g = math.log(1.0 - 2.0 ** (-5.0 - hd))
        q_decay = jnp.exp(row_p1 * log_g)
        k_decay = jnp.exp(row_p1 * (-log_g)) * k_scale
        chunk_decay = math.exp(log_g * RET_CHUNK)
        hcols = slice(hd * HEAD_DIM, (hd + 1) * HEAD_DIM)
        kcols = slice(group_w + hd * HEAD_DIM, group_w + (hd + 1) * HEAD_DIM)
        vcols = slice(2 * group_w + hd * HEAD_DIM, 2 * group_w + (hd + 1) * HEAD_DIM)
        gcols = slice(3 * group_w + hd * HEAD_DIM, 3 * group_w + (hd + 1) * HEAD_DIM)
        norm_g = ret_g_ref[:, hcols]
        for c in range(n_tok // RET_CHUNK):
            rows = slice(c * RET_CHUNK, (c + 1) * RET_CHUNK)
            cos_c, sin_c = cos_ref[rows, :], sin_ref[rows, :]
            q = proj_ref[rows, hcols]
            k = proj_ref[rows, kcols]
            vb = proj_ref[rows, vcols].astype(bf16)
            qd = ((q * cos_c + pltpu.roll(q, half, axis=1) * sin_c) * q_decay).astype(bf16)
            kd = (k * cos_c + pltpu.roll(k, half, axis=1) * sin_c) * k_decay
            kdt = kd.T.astype(bf16)
            scores = jnp.dot(qd, kdt, preferred_element_type=f32)
            p = jnp.where(causal, scores, 0.0).astype(bf16)
            s_prev = state_ref[hd]
            lhs = jnp.concatenate([p, qd], axis=1)
            rhs = jnp.concatenate([vb, s_prev.astype(bf16)], axis=0)
            o = jnp.dot(lhs, rhs, preferred_element_type=f32)
            kv = jnp.dot(kdt, vb, preferred_element_type=f32)
            state_ref[hd] = (s_prev + kv) * chunk_decay
            mu = jnp.mean(o, axis=-1, keepdims=True)
            dev = o - mu
            var = jnp.mean(dev * dev, axis=-1, keepdims=True)
            gate = _silu(proj_ref[rows, gcols])
            y_ref[rows, hcols] = (dev * jax.lax.rsqrt(var + EPS) * norm_g * gate).astype(bf16)

    u_off, gp_off = 4 * group_w, 5 * group_w
    pool_c = group_w // len(POOL_WINDOWS)
    hist_row = jax.lax.broadcasted_iota(jnp.int32, (POOL_HIST, pool_c), 0)
    t_head = (tile * n_tok + hist_row + 1).astype(f32)
    for g, w in enumerate(POOL_WINDOWS):
        ucols = slice(u_off + g * pool_c, u_off + (g + 1) * pool_c)
        gcols = slice(gp_off + g * pool_c, gp_off + (g + 1) * pool_c)
        pcols = slice(g * pool_c, (g + 1) * pool_c)
        u = proj_ref[:, ucols]
        s = jnp.concatenate([uhist_ref[:, pcols], u], axis=0)
        shift = 1
        while shift < w:
            s = s + pltpu.roll(s, shift, axis=0)
            shift *= 2
        wsum = s[POOL_HIST:]
        inv_head = 1.0 / jnp.minimum(t_head, float(w))
        mean = jnp.concatenate([wsum[:POOL_HIST] * inv_head, wsum[POOL_HIST:] * (1.0 / w)], axis=0)
        mixed = (mean - u).astype(bf16)
        pm = jnp.dot(mixed, pool_w_s[g * pool_c:(g + 1) * pool_c, :], preferred_element_type=f32)
        gate = _silu(proj_ref[:, gcols])
        y_ref[:, group_w + g * pool_c:group_w + (g + 1) * pool_c] = (
            pm * pool_scale_ref[:, pcols] * gate).astype(bf16)
    uhist_ref[...] = proj_ref[n_tok - POOL_HIST:, u_off:u_off + group_w]

    for r0 in range(0, n_tok, OUT_ROWS):
        rows = slice(r0, r0 + OUT_ROWS)
        out = jnp.dot(y_ref[rows, :], w_out_s[...], preferred_element_type=f32)
        ms_o = jnp.mean(out * out, axis=-1, keepdims=True)
        o_ref[rows, :] = x_ref[rows, :] + out * jax.lax.rsqrt(ms_o + EPS) * post_g_ref[...]


def _vmem_limit_bytes(d_model, d_proj, d_mix, pool_rows, pool_c, batch):
    scratch = (2 * (d_model * d_proj + d_mix * d_model + pool_rows * pool_c)
               + 4 * SEQ_TILE * d_proj
               + 2 * SEQ_TILE * d_mix
               + 6 * 4 * SEQ_TILE * HEAD_DIM
               + 4 * RET_HEADS * HEAD_DIM * HEAD_DIM + 4 * POOL_HIST * (d_mix // 2))
    blocks = 2 * (2 * 4 * SEQ_TILE * d_model + 2 * 4 * max(batch, 8) * SEQ_TILE)
    temporaries = 4 * SEQ_TILE * d_proj
    return scratch + blocks + temporaries


def _hybrid_layer(layer, x, positions, w_in, w_out, pool_w, pool_scale, ret_g, pre_g, post_g):
    batch, seq, d_model = x.shape
    d_proj = w_in.shape[2]
    d_mix = w_out.shape[1]
    group_w = d_mix // 2
    _, pool_rows, pool_c = pool_w.shape
    n_tiles = seq // SEQ_TILE
    assert seq % SEQ_TILE == 0 and SEQ_TILE % RET_CHUNK == 0 and SEQ_TILE % (2 * CAST_ROWS) == 0
    assert SEQ_TILE % OUT_ROWS == 0 and SEQ_TILE % LANES == 0
    assert group_w == RET_HEADS * HEAD_DIM and d_proj == N_PROJ * group_w
    assert d_mix % SEQ_TILE == 0 and pool_rows % SEQ_TILE == 0 and d_model % (SEQ_TILE // 2) == 0
    n_small = d_mix // SEQ_TILE + pool_rows // SEQ_TILE
    assert (d_mix // SEQ_TILE) * d_model + (pool_rows // SEQ_TILE) * pool_c <= d_proj
    assert n_small <= N_DMA_SEMS

    def pos_next_map(b, t):
        step = jnp.minimum(b * n_tiles + t + 1, batch * n_tiles - 1)
        return 0, step % n_tiles

    hbm = pl.BlockSpec(memory_space=pltpu.HBM)
    row_spec = pl.BlockSpec((1, group_w), lambda b, t: (0, 0))
    table = pltpu.VMEM((SEQ_TILE, HEAD_DIM), jnp.float32)
    return pl.pallas_call(
        functools.partial(_layer_kernel, layer),
        grid=(batch, n_tiles),
        in_specs=[
            pl.BlockSpec((None, SEQ_TILE, d_model), lambda b, t: (b, t, 0)),
            pl.BlockSpec((batch, SEQ_TILE), lambda b, t: (0, t)),
            pl.BlockSpec((batch, SEQ_TILE), pos_next_map),
            hbm, hbm, hbm,
            row_spec, row_spec,
            pl.BlockSpec((1, d_model), lambda b, t: (0, 0)),
            pl.BlockSpec((1, d_model), lambda b, t: (0, 0)),
        ],
        out_specs=pl.BlockSpec((None, SEQ_TILE, d_model), lambda b, t: (b, t, 0)),
        out_shape=jax.ShapeDtypeStruct(x.shape, x.dtype),
        scratch_shapes=[
            pltpu.VMEM((d_model, d_proj), jnp.bfloat16),
            pltpu.VMEM((d_mix, d_model), jnp.bfloat16),
            pltpu.VMEM((pool_rows, pool_c), jnp.bfloat16),
            pltpu.VMEM((SEQ_TILE, d_proj), jnp.float32),
            table, table,
            table, table,
            table, table,
            pltpu.VMEM((SEQ_TILE, d_mix), jnp.bfloat16),
            pltpu.VMEM((RET_HEADS, HEAD_DIM, HEAD_DIM), jnp.float32),
            pltpu.VMEM((POOL_HIST, group_w), jnp.float32),
            pltpu.SMEM((1,), jnp.int32),
            pltpu.SemaphoreType.DMA((N_DMA_SEMS,)),
        ],
        compiler_params=pltpu.CompilerParams(
            dimension_semantics=("arbitrary", "arbitrary"),
            vmem_limit_bytes=_vmem_limit_bytes(d_model, d_proj, d_mix, pool_rows, pool_c, batch)),
        name="hybrid_layer",
    )(x, positions, positions, w_in, w_out, pool_w, pool_scale, ret_g, pre_g, post_g)


@jax.jit
def kernel(x, positions, w_in, w_out, pool_w, pool_scale, ret_norm_g, pre_norm_g, post_norm_g):
    depth = w_in.shape[0]
    pool_w2d = pool_w.reshape(depth, -1, pool_w.shape[-1])
    for layer in range(depth):
        x = _hybrid_layer(
            layer, x, positions, w_in, w_out, pool_w2d,
            pool_scale[layer][None, :], ret_norm_g[layer][None, :],
            pre_norm_g[layer][None, :], post_norm_g[layer][None, :])
    return x
```

```python
import functools
import math

import jax
import jax.numpy as jnp
from jax.experimental import pallas as pl
from jax.experimental.pallas import tpu as pltpu

RET_HEADS = 8
HEAD_DIM = 128
ROPE_THETA = 10000.0
POOL_WINDOWS = (2, 4, 8, 16)
N_PROJ = 6
EPS = 1e-6

SEQ_TILE = 512
RET_CHUNK = 128
OUT_ROWS = 256
POOL_HIST = 16
CAST_ROWS = 16
LANES = 128
N_DMA_SEMS = 5


def _silu(g):
    return g * (1.0 / (1.0 + jnp.exp(-g)))


def _cast_rows(src_ref, src_row0, src_cols, dst_ref, dst_row0, dst_cols, n_rows):
    def body(i, carry):
        r = pl.multiple_of(i * CAST_ROWS, CAST_ROWS)
        dst_ref[pl.ds(dst_row0 + r, CAST_ROWS), dst_cols] = (
            src_ref[pl.ds(src_row0 + r, CAST_ROWS), src_cols].astype(jnp.bfloat16))
        return carry
    jax.lax.fori_loop(0, n_rows // CAST_ROWS, body, 0)


def _stage_weights(w_in_hbm, w_out_hbm, pool_w_hbm, w_in_s, w_out_s, pool_f32_ref, stage_ref, sems):
    n_tok = stage_ref.shape[0]
    d_model, d_proj = w_in_hbm.shape
    group_w = d_proj // N_PROJ
    pool_rows, pool_c = pool_w_hbm.shape
    n_out_chunks = w_out_hbm.shape[0] // n_tok

    def small_copy(i):
        if i < n_out_chunks:
            src = w_out_hbm.at[pl.ds(i * n_tok, n_tok), :]
            dst = stage_ref.at[:, pl.ds(i * d_model, d_model)]
        else:
            src, dst = pool_w_hbm, pool_f32_ref
        return pltpu.make_async_copy(src, dst, sems.at[i])

    n_small = n_out_chunks + 1
    for i in range(n_small):
        small_copy(i).start()
    for i in range(n_small):
        small_copy(i).wait()
    for i in range(n_out_chunks):
        _cast_rows(stage_ref, 0, slice(i * d_model, (i + 1) * d_model),
                   w_out_s, i * n_tok, slice(None), n_tok)

    half = n_tok // 2
    n_in_chunks = d_model // half
    u_off = 4 * group_w

    def in_copy(c):
        slot = c % 2
        return pltpu.make_async_copy(w_in_hbm.at[pl.ds(c * half, half), :],
                                     stage_ref.at[pl.ds(slot * half, half), :], sems.at[slot])

    in_copy(0).start()
    in_copy(1).start()
    for c in range(n_in_chunks):
        in_copy(c).wait()
        src_row0 = (c % 2) * half
        for cols in (slice(0, u_off), slice(u_off + group_w, d_proj)):
            _cast_rows(stage_ref, src_row0, cols, w_in_s, c * half, cols, half)
        for g in range(pool_rows // pool_c):
            cols = slice(u_off + g * pool_c, u_off + (g + 1) * pool_c)
            folded = jnp.dot(stage_ref[src_row0:src_row0 + half, cols],
                             pool_f32_ref[g * pool_c:(g + 1) * pool_c, :],
                             precision=jax.lax.Precision.HIGHEST,
                             preferred_element_type=jnp.float32)
            w_in_s[c * half:(c + 1) * half, cols] = folded.astype(jnp.bfloat16)
        if c + 2 < n_in_chunks:
            in_copy(c + 2).start()


def _lane_inv_freq():
    half = HEAD_DIM // 2
    lane = jax.lax.broadcasted_iota(jnp.int32, (1, HEAD_DIM), 1)
    freq_idx = (lane & (half - 1)).astype(jnp.float32)
    return jnp.exp(freq_idx * (-math.log(ROPE_THETA) / half))


def _offset_tables(cos_off_ref, sin_off_ref):
    n_tok = cos_off_ref.shape[0]
    offset = jax.lax.broadcasted_iota(jnp.int32, (n_tok, HEAD_DIM), 0).astype(jnp.float32)
    ang = offset * _lane_inv_freq()
    cos_off_ref[...] = jnp.cos(ang)
    sin_off_ref[...] = jnp.sin(ang)


def _general_tables(pos_ref, batch_row, cos_ref, sin_ref):
    f32 = jnp.float32
    n_tok = pos_ref.shape[1]
    half = HEAD_DIM // 2
    freq_idx = jax.lax.broadcasted_iota(jnp.int32, (half, LANES), 0).astype(f32)
    inv_freq = jnp.exp(freq_idx * (-math.log(ROPE_THETA) / half))
    inv_freq = jnp.concatenate([inv_freq] * (n_tok // LANES), axis=1)
    pos = pos_ref[pl.ds(batch_row, 1), :].astype(f32)
    ang = inv_freq * pos
    cos_a, sin_a = jnp.cos(ang), jnp.sin(ang)
    cos_ref[...] = jnp.concatenate([cos_a, cos_a], axis=0).T
    sin_ref[...] = jnp.concatenate([-sin_a, sin_a], axis=0).T


def _is_contiguous(pos_ref, batch_row):
    pos = pos_ref[pl.ds(batch_row, 1), :]
    offset = jax.lax.broadcasted_iota(jnp.int32, pos.shape, 1)
    return jnp.min((pos - pos[:, 0:1] == offset).astype(jnp.int32))


def _contiguous_tables(pos_ref, batch_row, cos_off_ref, sin_off_ref):
    half = HEAD_DIM // 2
    lane = jax.lax.broadcasted_iota(jnp.int32, (1, HEAD_DIM), 1)
    first = pos_ref[pl.ds(batch_row, 1), :][:, 0:1].astype(jnp.float32)
    base = first * _lane_inv_freq()
    cos_b, sin_b = jnp.cos(base), jnp.sin(base)
    sign = jnp.where(lane < half, -1.0, 1.0)
    cos_o, sin_o = cos_off_ref[...], sin_off_ref[...]
    cos_t = cos_b * cos_o - sin_b * sin_o
    sin_t = (sign * sin_b) * cos_o + (sign * cos_b) * sin_o
    return cos_t, sin_t


def _layer_kernel(layer, x_ref, pos_ref, posn_ref, w_in_all, w_out_all, pool_w_all,
                  pool_scale_ref, ret_g_ref, pre_g_ref, post_g_ref, o_ref,
                  w_in_s, w_out_s, pool_f32_ref, proj_ref, cos_off_ref, sin_off_ref,
                  cos_gen_ref, sin_gen_ref, cos_ref, sin_ref, y_ref,
                  state_ref, uhist_ref, contig_ref, sems):
    f32, bf16 = jnp.float32, jnp.bfloat16
    w_in_hbm, w_out_hbm, pool_w_hbm = w_in_all.at[layer], w_out_all.at[layer], pool_w_all.at[layer]
    b, tile = pl.program_id(0), pl.program_id(1)
    n_tiles = pl.num_programs(1)
    n_tok, d_model = x_ref.shape
    group_w = RET_HEADS * HEAD_DIM
    half = HEAD_DIM // 2

    @pl.when((b == 0) & (tile == 0))
    def _():
        _stage_weights(w_in_hbm, w_out_hbm, pool_w_hbm, w_in_s, w_out_s, pool_f32_ref, proj_ref, sems)
        _offset_tables(cos_off_ref, sin_off_ref)
        cos_gen_ref[...] = jnp.zeros_like(cos_gen_ref)
        sin_gen_ref[...] = jnp.zeros_like(sin_gen_ref)
        contig_ref[0] = _is_contiguous(pos_ref, b)

    @pl.when(tile == 0)
    def _():
        state_ref[...] = jnp.zeros_like(state_ref)
        uhist_ref[...] = jnp.zeros_like(uhist_ref)

    contiguous = contig_ref[0]

    @pl.when(contiguous == 0)
    def _():
        _general_tables(pos_ref, b, cos_gen_ref, sin_gen_ref)

    wraps = tile + 1 == n_tiles
    next_row = jnp.minimum(b + wraps.astype(jnp.int32), pl.num_programs(0) - 1)
    contig_ref[0] = _is_contiguous(posn_ref, next_row)

    x = x_ref[...]
    ms = jnp.mean(x * x, axis=-1, keepdims=True)
    hb = (x * jax.lax.rsqrt(ms + EPS) * pre_g_ref[...]).astype(bf16)
    for seg in range(N_PROJ):
        cols = slice(seg * group_w, (seg + 1) * group_w)
        proj_ref[:, cols] = jnp.dot(hb, w_in_s[:, cols], preferred_element_type=f32)

    use_fast = jnp.broadcast_to(contiguous, (n_tok, HEAD_DIM)) != 0
    cos_fast, sin_fast = _contiguous_tables(pos_ref, b, cos_off_ref, sin_off_ref)
    cos_ref[...] = jnp.where(use_fast, cos_fast, cos_gen_ref[...])
    sin_ref[...] = jnp.where(use_fast, sin_fast, sin_gen_ref[...])

    row = jax.lax.broadcasted_iota(jnp.int32, (RET_CHUNK, HEAD_DIM), 0)
    col = jax.lax.broadcasted_iota(jnp.int32, (RET_CHUNK, HEAD_DIM), 1)
    causal = row >= col
    row_p1 = (row + 1).astype(f32)
    k_scale = HEAD_DIM ** -0.5
    for hd in range(RET_HEADS):
        log_g = math.log(1.0 - 2.0 ** (-5.0 - hd))
        q_decay = jnp.exp(row_p1 * log_g)
        k_decay = jnp.exp(row_p1 * (-log_g)) * k_scale
        chunk_decay = math.exp(log_g * RET_CHUNK)
        hcols = slice(hd * HEAD_DIM, (hd + 1) * HEAD_DIM)
        kcols = slice(group_w + hd * HEAD_DIM, group_w + (hd + 1) * HEAD_DIM)
        vcols = slice(2 * group_w + hd * HEAD_DIM, 2 * group_w + (hd + 1) * HEAD_DIM)
        gcols = slice(3 * group_w + hd * HEAD_DIM, 3 * group_w + (hd + 1) * HEAD_DIM)
        norm_g = ret_g_ref[:, hcols]
        for c in range(n_tok // RET_CHUNK):
            rows = slice(c * RET_CHUNK, (c + 1) * RET_CHUNK)
            cos_c, sin_c = cos_ref[rows, :], sin_ref[rows, :]
            q = proj_ref[rows, hcols]
            k = proj_ref[rows, kcols]
            vb = proj_ref[rows, vcols].astype(bf16)
            qd = ((q * cos_c + pltpu.roll(q, half, axis=1) * sin_c) * q_decay).astype(bf16)
            kd = (k * cos_c + pltpu.roll(k, half, axis=1) * sin_c) * k_decay
            kdt = kd.T.astype(bf16)
            scores = jnp.dot(qd, kdt, preferred_element_type=f32)
            p = jnp.where(causal, scores, 0.0).astype(bf16)
            s_prev = state_ref[hd]
            lhs = jnp.concatenate([p, qd], axis=1)
            rhs = jnp.concatenate([vb, s_prev.astype(bf16)], axis=0)
            o = jnp.dot(lhs, rhs, preferred_element_type=f32)
            kv = jnp.dot(kdt, vb, preferred_element_type=f32)
            state_ref[hd] = (s_prev + kv) * chunk_decay
            mu = jnp.mean(o, axis=-1, keepdims=True)
            dev = o - mu
            var = jnp.mean(dev * dev, axis=-1, keepdims=True)
            gate = _silu(proj_ref[rows, gcols])
            y_ref[rows, hcols] = (dev * jax.lax.rsqrt(var + EPS) * norm_g * gate).astype(bf16)

    u_off, gp_off = 4 * group_w, 5 * group_w
    pool_c = group_w // len(POOL_WINDOWS)
    hist_row = jax.lax.broadcasted_iota(jnp.int32, (POOL_HIST, pool_c), 0)
    t_head = (tile * n_tok + hist_row + 1).astype(f32)
    for g, w in enumerate(POOL_WINDOWS):
        ucols = slice(u_off + g * pool_c, u_off + (g + 1) * pool_c)
        gcols = slice(gp_off + g * pool_c, gp_off + (g + 1) * pool_c)
        pcols = slice(g * pool_c, (g + 1) * pool_c)
        u = proj_ref[:, ucols]
        s = jnp.concatenate([uhist_ref[:, pcols], u], axis=0)
        shift = 1
        while shift < w:
            s = s + pltpu.roll(s, shift, axis=0)
            shift *= 2
        wsum = s[POOL_HIST:]
        inv_head = 1.0 / jnp.minimum(t_head, float(w))
        mean = jnp.concatenate([wsum[:POOL_HIST] * inv_head, wsum[POOL_HIST:] * (1.0 / w)], axis=0)
        gate = _silu(proj_ref[:, gcols])
        y_ref[:, group_w + g * pool_c:group_w + (g + 1) * pool_c] = (
            (mean - u) * pool_scale_ref[:, pcols] * gate).astype(bf16)
    uhist_ref[...] = proj_ref[n_tok - POOL_HIST:, u_off:u_off + group_w]

    for r0 in range(0, n_tok, OUT_ROWS):
        rows = slice(r0, r0 + OUT_ROWS)
        out = jnp.dot(y_ref[rows, :], w_out_s[...], preferred_element_type=f32)
        ms_o = jnp.mean(out * out, axis=-1, keepdims=True)
        o_ref[rows, :] = x_ref[rows, :] + out * jax.lax.rsqrt(ms_o + EPS) * post_g_ref[...]


def _vmem_limit_bytes(d_model, d_proj, d_mix, pool_rows, pool_c, batch):
    scratch = (2 * (d_model * d_proj + d_mix * d_model) + 4 * pool_rows * pool_c
               + 4 * SEQ_TILE * d_proj
               + 2 * SEQ_TILE * d_mix
               + 6 * 4 * SEQ_TILE * HEAD_DIM
               + 4 * RET_HEADS * HEAD_DIM * HEAD_DIM + 4 * POOL_HIST * (d_mix // 2))
    blocks = 2 * (2 * 4 * SEQ_TILE * d_model + 2 * 4 * max(batch, 8) * SEQ_TILE)
    temporaries = 4 * SEQ_TILE * d_proj
    return scratch + blocks + temporaries


def _hybrid_layer(layer, x, positions, w_in, w_out, pool_w, pool_scale, ret_g, pre_g, post_g):
    batch, seq, d_model = x.shape
    d_proj = w_in.shape[2]
    d_mix = w_out.shape[1]
    group_w = d_mix // 2
    _, pool_rows, pool_c = pool_w.shape
    n_tiles = seq // SEQ_TILE
    assert seq % SEQ_TILE == 0 and SEQ_TILE % RET_CHUNK == 0 and SEQ_TILE % (2 * CAST_ROWS) == 0
    assert SEQ_TILE % OUT_ROWS == 0 and SEQ_TILE % LANES == 0
    assert group_w == RET_HEADS * HEAD_DIM and d_proj == N_PROJ * group_w
    assert pool_rows == group_w and pool_rows % pool_c == 0
    assert d_mix % SEQ_TILE == 0 and d_model % (SEQ_TILE // 2) == 0
    assert (d_mix // SEQ_TILE) * d_model <= d_proj and d_mix // SEQ_TILE + 1 <= N_DMA_SEMS

    def pos_next_map(b, t):
        step = jnp.minimum(b * n_tiles + t + 1, batch * n_tiles - 1)
        return 0, step % n_tiles

    hbm = pl.BlockSpec(memory_space=pltpu.HBM)
    row_spec = pl.BlockSpec((1, group_w), lambda b, t: (0, 0))
    table = pltpu.VMEM((SEQ_TILE, HEAD_DIM), jnp.float32)
    return pl.pallas_call(
        functools.partial(_layer_kernel, layer),
        grid=(batch, n_tiles),
        in_specs=[
            pl.BlockSpec((None, SEQ_TILE, d_model), lambda b, t: (b, t, 0)),
            pl.BlockSpec((batch, SEQ_TILE), lambda b, t: (0, t)),
            pl.BlockSpec((batch, SEQ_TILE), pos_next_map),
            hbm, hbm, hbm,
            row_spec, row_spec,
            pl.BlockSpec((1, d_model), lambda b, t: (0, 0)),
            pl.BlockSpec((1, d_model), lambda b, t: (0, 0)),
        ],
        out_specs=pl.BlockSpec((None, SEQ_TILE, d_model), lambda b, t: (b, t, 0)),
        out_shape=jax.ShapeDtypeStruct(x.shape, x.dtype),
        scratch_shapes=[
            pltpu.VMEM((d_model, d_proj), jnp.bfloat16),
            pltpu.VMEM((d_mix, d_model), jnp.bfloat16),
            pltpu.VMEM((pool_rows, pool_c), jnp.float32),
            pltpu.VMEM((SEQ_TILE, d_proj), jnp.float32),
            table, table,
            table, table,
            table, table,
            pltpu.VMEM((SEQ_TILE, d_mix), jnp.bfloat16),
            pltpu.VMEM((RET_HEADS, HEAD_DIM, HEAD_DIM), jnp.float32),
            pltpu.VMEM((POOL_HIST, group_w), jnp.float32),
            pltpu.SMEM((1,), jnp.int32),
            pltpu.SemaphoreType.DMA((N_DMA_SEMS,)),
        ],
        compiler_params=pltpu.CompilerParams(
            dimension_semantics=("arbitrary", "arbitrary"),
            vmem_limit_bytes=_vmem_limit_bytes(d_model, d_proj, d_mix, pool_rows, pool_c, batch)),
        name="hybrid_layer",
    )(x, positions, positions, w_in, w_out, pool_w, pool_scale, ret_g, pre_g, post_g)


@jax.jit
def kernel(x, positions, w_in, w_out, pool_w, pool_scale, ret_norm_g, pre_norm_g, post_norm_g):
    depth = w_in.shape[0]
    pool_w2d = pool_w.reshape(depth, -1, pool_w.shape[-1])
    for layer in range(depth):
        x = _hybrid_layer(
            layer, x, positions, w_in, w_out, pool_w2d,
            pool_scale[layer][None, :], ret_norm_g[layer][None, :],
            pre_norm_g[layer][None, :], post_norm_g[layer][None, :])
    return x
```

```python
import functools
import math

import jax
import jax.numpy as jnp
from jax.experimental import pallas as pl
from jax.experimental.pallas import tpu as pltpu

RET_HEADS = 8
HEAD_DIM = 128
ROPE_THETA = 10000.0
POOL_WINDOWS = (2, 4, 8, 16)
N_PROJ = 6
EPS = 1e-6

SEQ_TILE = 512
RET_CHUNK = 128
OUT_ROWS = 256
POOL_HIST = 16
CAST_ROWS = 16
LANES = 128
N_DMA_SEMS = 5


def _silu(g):
    return g * (1.0 / (1.0 + jnp.exp(-g)))


def _cast_rows(src_ref, src_row0, src_cols, dst_ref, dst_row0, dst_cols, n_rows):
    def body(i, carry):
        r = pl.multiple_of(i * CAST_ROWS, CAST_ROWS)
        dst_ref[pl.ds(dst_row0 + r, CAST_ROWS), dst_cols] = (
            src_ref[pl.ds(src_row0 + r, CAST_ROWS), src_cols].astype(jnp.bfloat16))
        return carry
    jax.lax.fori_loop(0, n_rows // CAST_ROWS, body, 0)


def _stage_weights(w_in_hbm, w_out_hbm, pool_w_hbm, w_in_s, w_out_s, pool_f32_ref, stage_ref, sems):
    n_tok = stage_ref.shape[0]
    d_model, d_proj = w_in_hbm.shape
    group_w = d_proj // N_PROJ
    pool_rows, pool_c = pool_w_hbm.shape
    n_out_chunks = w_out_hbm.shape[0] // n_tok

    def small_copy(i):
        if i < n_out_chunks:
            src = w_out_hbm.at[pl.ds(i * n_tok, n_tok), :]
            dst = stage_ref.at[:, pl.ds(i * d_model, d_model)]
        else:
            src, dst = pool_w_hbm, pool_f32_ref
        return pltpu.make_async_copy(src, dst, sems.at[i])

    n_small = n_out_chunks + 1
    for i in range(n_small):
        small_copy(i).start()
    for i in range(n_small):
        small_copy(i).wait()
    for i in range(n_out_chunks):
        _cast_rows(stage_ref, 0, slice(i * d_model, (i + 1) * d_model),
                   w_out_s, i * n_tok, slice(None), n_tok)

    half = n_tok // 2
    n_in_chunks = d_model // half
    u_off = 4 * group_w

    def in_copy(c):
        slot = c % 2
        return pltpu.make_async_copy(w_in_hbm.at[pl.ds(c * half, half), :],
                                     stage_ref.at[pl.ds(slot * half, half), :], sems.at[slot])

    in_copy(0).start()
    in_copy(1).start()
    for c in range(n_in_chunks):
        in_copy(c).wait()
        src_row0 = (c % 2) * half
        for cols in (slice(0, u_off), slice(u_off + group_w, d_proj)):
            _cast_rows(stage_ref, src_row0, cols, w_in_s, c * half, cols, half)
        for g in range(pool_rows // pool_c):
            cols = slice(u_off + g * pool_c, u_off + (g + 1) * pool_c)
            folded = jnp.dot(stage_ref[src_row0:src_row0 + half, cols].astype(jnp.bfloat16),
                             pool_f32_ref[g * pool_c:(g + 1) * pool_c, :].astype(jnp.bfloat16),
                             preferred_element_type=jnp.float32)
            w_in_s[c * half:(c + 1) * half, cols] = folded.astype(jnp.bfloat16)
        if c + 2 < n_in_chunks:
            in_copy(c + 2).start()


def _lane_inv_freq():
    half = HEAD_DIM // 2
    lane = jax.lax.broadcasted_iota(jnp.int32, (1, HEAD_DIM), 1)
    freq_idx = (lane & (half - 1)).astype(jnp.float32)
    return jnp.exp(freq_idx * (-math.log(ROPE_THETA) / half))


def _offset_tables(cos_off_ref, sin_off_ref):
    n_tok = cos_off_ref.shape[0]
    offset = jax.lax.broadcasted_iota(jnp.int32, (n_tok, HEAD_DIM), 0).astype(jnp.float32)
    ang = offset * _lane_inv_freq()
    cos_off_ref[...] = jnp.cos(ang)
    sin_off_ref[...] = jnp.sin(ang)


def _general_tables(pos_ref, batch_row, cos_ref, sin_ref):
    f32 = jnp.float32
    n_tok = pos_ref.shape[1]
    half = HEAD_DIM // 2
    freq_idx = jax.lax.broadcasted_iota(jnp.int32, (half, LANES), 0).astype(f32)
    inv_freq = jnp.exp(freq_idx * (-math.log(ROPE_THETA) / half))
    inv_freq = jnp.concatenate([inv_freq] * (n_tok // LANES), axis=1)
    pos = pos_ref[pl.ds(batch_row, 1), :].astype(f32)
    ang = inv_freq * pos
    cos_a, sin_a = jnp.cos(ang), jnp.sin(ang)
    cos_ref[...] = jnp.concatenate([cos_a, cos_a], axis=0).T
    sin_ref[...] = jnp.concatenate([-sin_a, sin_a], axis=0).T


def _is_contiguous(pos_ref, batch_row):
    pos = pos_ref[pl.ds(batch_row, 1), :]
    offset = jax.lax.broadcasted_iota(jnp.int32, pos.shape, 1)
    return jnp.min((pos - pos[:, 0:1] == offset).astype(jnp.int32))


def _contiguous_tables(pos_ref, batch_row, cos_off_ref, sin_off_ref):
    half = HEAD_DIM // 2
    lane = jax.lax.broadcasted_iota(jnp.int32, (1, HEAD_DIM), 1)
    first = pos_ref[pl.ds(batch_row, 1), :][:, 0:1].astype(jnp.float32)
    base = first * _lane_inv_freq()
    cos_b, sin_b = jnp.cos(base), jnp.sin(base)
    sign = jnp.where(lane < half, -1.0, 1.0)
    cos_o, sin_o = cos_off_ref[...], sin_off_ref[...]
    cos_t = cos_b * cos_o - sin_b * sin_o
    sin_t = (sign * sin_b) * cos_o + (sign * cos_b) * sin_o
    return cos_t, sin_t


def _layer_kernel(layer, x_ref, pos_ref, posn_ref, w_in_all, w_out_all, pool_w_all,
                  pool_scale_ref, ret_g_ref, pre_g_ref, post_g_ref, o_ref,
                  w_in_s, w_out_s, pool_f32_ref, proj_ref, cos_off_ref, sin_off_ref,
                  cos_gen_ref, sin_gen_ref, cos_ref, sin_ref, y_ref,
                  state_ref, uhist_ref, contig_ref, sems):
    f32, bf16 = jnp.float32, jnp.bfloat16
    w_in_hbm, w_out_hbm, pool_w_hbm = w_in_all.at[layer], w_out_all.at[layer], pool_w_all.at[layer]
    b, tile = pl.program_id(0), pl.program_id(1)
    n_tiles = pl.num_programs(1)
    n_tok, d_model = x_ref.shape
    group_w = RET_HEADS * HEAD_DIM
    half = HEAD_DIM // 2

    @pl.when((b == 0) & (tile == 0))
    def _():
        _stage_weights(w_in_hbm, w_out_hbm, pool_w_hbm, w_in_s, w_out_s, pool_f32_ref, proj_ref, sems)
        _offset_tables(cos_off_ref, sin_off_ref)
        cos_gen_ref[...] = jnp.zeros_like(cos_gen_ref)
        sin_gen_ref[...] = jnp.zeros_like(sin_gen_ref)
        contig_ref[0] = _is_contiguous(pos_ref, b)

    @pl.when(tile == 0)
    def _():
        state_ref[...] = jnp.zeros_like(state_ref)
        uhist_ref[...] = jnp.zeros_like(uhist_ref)

    contiguous = contig_ref[0]

    @pl.when(contiguous == 0)
    def _():
        _general_tables(pos_ref, b, cos_gen_ref, sin_gen_ref)

    wraps = tile + 1 == n_tiles
    next_row = jnp.minimum(b + wraps.astype(jnp.int32), pl.num_programs(0) - 1)
    contig_ref[0] = _is_contiguous(posn_ref, next_row)

    x = x_ref[...]
    ms = jnp.mean(x * x, axis=-1, keepdims=True)
    hb = (x * jax.lax.rsqrt(ms + EPS) * pre_g_ref[...]).astype(bf16)
    for seg in range(N_PROJ):
        cols = slice(seg * group_w, (seg + 1) * group_w)
        proj_ref[:, cols] = jnp.dot(hb, w_in_s[:, cols], preferred_element_type=f32)

    use_fast = jnp.broadcast_to(contiguous, (n_tok, HEAD_DIM)) != 0
    cos_fast, sin_fast = _contiguous_tables(pos_ref, b, cos_off_ref, sin_off_ref)
    cos_ref[...] = jnp.where(use_fast, cos_fast, cos_gen_ref[...])
    sin_ref[...] = jnp.where(use_fast, sin_fast, sin_gen_ref[...])

    row = jax.lax.broadcasted_iota(jnp.int32, (RET_CHUNK, HEAD_DIM), 0)
    col = jax.lax.broadcasted_iota(jnp.int32, (RET_CHUNK, HEAD_DIM), 1)
    causal = row >= col
    row_p1 = (row + 1).astype(f32)
    k_scale = HEAD_DIM ** -0.5
    for hd in range(RET_HEADS):
        log_g = math.log(1.0 - 2.0 ** (-5.0 - hd))
        q_decay = jnp.exp(row_p1 * log_g)
        k_decay = jnp.exp(row_p1 * (-log_g)) * k_scale
        chunk_decay = math.exp(log_g * RET_CHUNK)
        hcols = slice(hd * HEAD_DIM, (hd + 1) * HEAD_DIM)
        kcols = slice(group_w + hd * HEAD_DIM, group_w + (hd + 1) * HEAD_DIM)
        vcols = slice(2 * group_w + hd * HEAD_DIM, 2 * group_w + (hd + 1) * HEAD_DIM)
        gcols = slice(3 * group_w + hd * HEAD_DIM, 3 * group_w + (hd + 1) * HEAD_DIM)
        norm_g = ret_g_ref[:, hcols]
        for c in range(n_tok // RET_CHUNK):
            rows = slice(c * RET_CHUNK, (c + 1) * RET_CHUNK)
            cos_c, sin_c = cos_ref[rows, :], sin_ref[rows, :]
            q = proj_ref[rows, hcols]
            k = proj_ref[rows, kcols]
            vb = proj_ref[rows, vcols].astype(bf16)
            qd = ((q * cos_c + pltpu.roll(q, half, axis=1) * sin_c) * q_decay).astype(bf16)
            kd = (k * cos_c + pltpu.roll(k, half, axis=1) * sin_c) * k_decay
            kdt = kd.T.astype(bf16)
            scores = jnp.dot(qd, kdt, preferred_element_type=f32)
            p = jnp.where(causal, scores, 0.0).astype(bf16)
            s_prev = state_ref[hd]
            lhs = jnp.concatenate([p, qd], axis=1)
            rhs = jnp.concatenate([vb, s_prev.astype(bf16)], axis=0)
            o = jnp.dot(lhs, rhs, preferred_element_type=f32)
            kv = jnp.dot(kdt, vb, preferred_element_type=f32)
            state_ref[hd] = (s_prev + kv) * chunk_decay
            mu = jnp.mean(o, axis=-1, keepdims=True)
            dev = o - mu
            var = jnp.mean(dev * dev, axis=-1, keepdims=True)
            gate = _silu(proj_ref[rows, gcols])
            y_ref[rows, hcols] = (dev * jax.lax.rsqrt(var + EPS) * norm_g * gate).astype(bf16)

    u_off, gp_off = 4 * group_w, 5 * group_w
    pool_c = group_w // len(POOL_WINDOWS)
    hist_row = jax.lax.broadcasted_iota(jnp.int32, (POOL_HIST, pool_c), 0)
    t_head = (tile * n_tok + hist_row + 1).astype(f32)
    for g, w in enumerate(POOL_WINDOWS):
        ucols = slice(u_off + g * pool_c, u_off + (g + 1) * pool_c)
        gcols = slice(gp_off + g * pool_c, gp_off + (g + 1) * pool_c)
        pcols = slice(g * pool_c, (g + 1) * pool_c)
        u = proj_ref[:, ucols]
        s = jnp.concatenate([uhist_ref[:, pcols], u], axis=0)
        shift = 1
        while shift < w:
            s = s + pltpu.roll(s, shift, axis=0)
            shift *= 2
        wsum = s[POOL_HIST:]
        inv_head = 1.0 / jnp.minimum(t_head, float(w))
        mean = jnp.concatenate([wsum[:POOL_HIST] * inv_head, wsum[POOL_HIST:] * (1.0 / w)], axis=0)
        gate = _silu(proj_ref[:, gcols])
        y_ref[:, group_w + g * pool_c:group_w + (g + 1) * pool_c] = (
            (mean - u) * pool_scale_ref[:, pcols] * gate).astype(bf16)
    uhist_ref[...] = proj_ref[n_tok - POOL_HIST:, u_off:u_off + group_w]

    for r0 in range(0, n_tok, OUT_ROWS):
        rows = slice(r0, r0 + OUT_ROWS)
        out = jnp.dot(y_ref[rows, :], w_out_s[...], preferred_element_type=f32)
        ms_o = jnp.mean(out * out, axis=-1, keepdims=True)
        o_ref[rows, :] = x_ref[rows, :] + out * jax.lax.rsqrt(ms_o + EPS) * post_g_ref[...]


def _vmem_limit_bytes(d_model, d_proj, d_mix, pool_rows, pool_c, batch):
    scratch = (2 * (d_model * d_proj + d_mix * d_model) + 4 * pool_rows * pool_c
               + 4 * SEQ_TILE * d_proj
               + 2 * SEQ_TILE * d_mix
               + 6 * 4 * SEQ_TILE * HEAD_DIM
               + 4 * RET_HEADS * HEAD_DIM * HEAD_DIM + 4 * POOL_HIST * (d_mix // 2))
    blocks = 2 * (2 * 4 * SEQ_TILE * d_model + 2 * 4 * max(batch, 8) * SEQ_TILE)
    temporaries = 4 * SEQ_TILE * d_proj
    return scratch + blocks + temporaries


def _hybrid_layer(layer, x, positions, w_in, w_out, pool_w, pool_scale, ret_g, pre_g, post_g):
    batch, seq, d_model = x.shape
    d_proj = w_in.shape[2]
    d_mix = w_out.shape[1]
    group_w = d_mix // 2
    _, pool_rows, pool_c = pool_w.shape
    n_tiles = seq // SEQ_TILE
    assert seq % SEQ_TILE == 0 and SEQ_TILE % RET_CHUNK == 0 and SEQ_TILE % (2 * CAST_ROWS) == 0
    assert SEQ_TILE % OUT_ROWS == 0 and SEQ_TILE % LANES == 0
    assert group_w == RET_HEADS * HEAD_DIM and d_proj == N_PROJ * group_w
    assert pool_rows == group_w and pool_rows % pool_c == 0
    assert d_mix % SEQ_TILE == 0 and d_model % (SEQ_TILE // 2) == 0
    assert (d_mix // SEQ_TILE) * d_model <= d_proj and d_mix // SEQ_TILE + 1 <= N_DMA_SEMS

    def pos_next_map(b, t):
        step = jnp.minimum(b * n_tiles + t + 1, batch * n_tiles - 1)
        return 0, step % n_tiles

    hbm = pl.BlockSpec(memory_space=pltpu.HBM)
    row_spec = pl.BlockSpec((1, group_w), lambda b, t: (0, 0))
    table = pltpu.VMEM((SEQ_TILE, HEAD_DIM), jnp.float32)
    return pl.pallas_call(
        functools.partial(_layer_kernel, layer),
        grid=(batch, n_tiles),
        in_specs=[
            pl.BlockSpec((None, SEQ_TILE, d_model), lambda b, t: (b, t, 0)),
            pl.BlockSpec((batch, SEQ_TILE), lambda b, t: (0, t)),
            pl.BlockSpec((batch, SEQ_TILE), pos_next_map),
            hbm, hbm, hbm,
            row_spec, row_spec,
            pl.BlockSpec((1, d_model), lambda b, t: (0, 0)),
            pl.BlockSpec((1, d_model), lambda b, t: (0, 0)),
        ],
        out_specs=pl.BlockSpec((None, SEQ_TILE, d_model), lambda b, t: (b, t, 0)),
        out_shape=jax.ShapeDtypeStruct(x.shape, x.dtype),
        scratch_shapes=[
            pltpu.VMEM((d_model, d_proj), jnp.bfloat16),
            pltpu.VMEM((d_mix, d_model), jnp.bfloat16),
            pltpu.VMEM((pool_rows, pool_c), jnp.float32),
            pltpu.VMEM((SEQ_TILE, d_proj), jnp.float32),
            table, table,
            table, table,
            table, table,
            pltpu.VMEM((SEQ_TILE, d_mix), jnp.bfloat16),
            pltpu.VMEM((RET_HEADS, HEAD_DIM, HEAD_DIM), jnp.float32),
            pltpu.VMEM((POOL_HIST, group_w), jnp.float32),
            pltpu.SMEM((1,), jnp.int32),
            pltpu.SemaphoreType.DMA((N_DMA_SEMS,)),
        ],
        compiler_params=pltpu.CompilerParams(
            dimension_semantics=("arbitrary", "arbitrary"),
            vmem_limit_bytes=_vmem_limit_bytes(d_model, d_proj, d_mix, pool_rows, pool_c, batch)),
        name="hybrid_layer",
    )(x, positions, positions, w_in, w_out, pool_w, pool_scale, ret_g, pre_g, post_g)


@jax.jit
def kernel(x, positions, w_in, w_out, pool_w, pool_scale, ret_norm_g, pre_norm_g, post_norm_g):
    depth = w_in.shape[0]
    pool_w2d = pool_w.reshape(depth, -1, pool_w.shape[-1])
    for layer in range(depth):
        x = _hybrid_layer(
            layer, x, positions, w_in, w_out, pool_w2d,
            pool_scale[layer][None, :], ret_norm_g[layer][None, :],
            pre_norm_g[layer][None, :], post_norm_g[layer][None, :])
    return x
```

```python
import functools
import math

import jax
import jax.numpy as jnp
from jax.experimental import pallas as pl
from jax.experimental.pallas import tpu as pltpu

RET_HEADS = 8
HEAD_DIM = 128
ROPE_THETA = 10000.0
POOL_WINDOWS = (2, 4, 8, 16)
N_PROJ = 6
EPS = 1e-6

SEQ_TILE = 512
RET_CHUNK = 128
OUT_ROWS = 256
POOL_HIST = 16
CAST_ROWS = 16
LANES = 128
N_DMA_SEMS = 5


def _silu(g):
    return g * (1.0 / (1.0 + jnp.exp(-g)))


def _cast_rows(src_ref, src_row0, src_cols, dst_ref, dst_row0, dst_cols, n_rows):
    def body(i, carry):
        r = pl.multiple_of(i * CAST_ROWS, CAST_ROWS)
        dst_ref[pl.ds(dst_row0 + r, CAST_ROWS), dst_cols] = (
            src_ref[pl.ds(src_row0 + r, CAST_ROWS), src_cols].astype(jnp.bfloat16))
        return carry
    jax.lax.fori_loop(0, n_rows // CAST_ROWS, body, 0)


def _stage_weights(w_in_hbm, w_out_hbm, pool_w_hbm, w_in_s, w_out_s, pool_f32_ref, stage_ref, sems):
    n_tok = stage_ref.shape[0]
    d_model, d_proj = w_in_hbm.shape
    group_w = d_proj // N_PROJ
    pool_rows, pool_c = pool_w_hbm.shape
    n_out_chunks = w_out_hbm.shape[0] // n_tok

    def small_copy(i):
        if i < n_out_chunks:
            src = w_out_hbm.at[pl.ds(i * n_tok, n_tok), :]
            dst = stage_ref.at[:, pl.ds(i * d_model, d_model)]
        else:
            src, dst = pool_w_hbm, pool_f32_ref
        return pltpu.make_async_copy(src, dst, sems.at[i])

    n_small = n_out_chunks + 1
    for i in range(n_small):
        small_copy(i).start()
    for i in range(n_small):
        small_copy(i).wait()
    for i in range(n_out_chunks):
        _cast_rows(stage_ref, 0, slice(i * d_model, (i + 1) * d_model),
                   w_out_s, i * n_tok, slice(None), n_tok)

    half = n_tok // 2
    n_in_chunks = d_model // half
    u_off = 4 * group_w

    def in_copy(c):
        slot = c % 2
        return pltpu.make_async_copy(w_in_hbm.at[pl.ds(c * half, half), :],
                                     stage_ref.at[pl.ds(slot * half, half), :], sems.at[slot])

    in_copy(0).start()
    in_copy(1).start()
    for c in range(n_in_chunks):
        in_copy(c).wait()
        src_row0 = (c % 2) * half
        for cols in (slice(0, u_off), slice(u_off + group_w, d_proj)):
            _cast_rows(stage_ref, src_row0, cols, w_in_s, c * half, cols, half)
        for g in range(pool_rows // pool_c):
            cols = slice(u_off + g * pool_c, u_off + (g + 1) * pool_c)
            folded = jnp.dot(stage_ref[src_row0:src_row0 + half, cols].astype(jnp.bfloat16),
                             pool_f32_ref[g * pool_c:(g + 1) * pool_c, :].astype(jnp.bfloat16),
                             preferred_element_type=jnp.float32)
            w_in_s[c * half:(c + 1) * half, cols] = folded.astype(jnp.bfloat16)
        if c + 2 < n_in_chunks:
            in_copy(c + 2).start()


def _lane_inv_freq():
    half = HEAD_DIM // 2
    lane = jax.lax.broadcasted_iota(jnp.int32, (1, HEAD_DIM), 1)
    freq_idx = (lane & (half - 1)).astype(jnp.float32)
    return jnp.exp(freq_idx * (-math.log(ROPE_THETA) / half))


def _offset_tables(cos_off_ref, sin_off_ref):
    n_tok = cos_off_ref.shape[0]
    offset = jax.lax.broadcasted_iota(jnp.int32, (n_tok, HEAD_DIM), 0).astype(jnp.float32)
    ang = offset * _lane_inv_freq()
    cos_off_ref[...] = jnp.cos(ang)
    sin_off_ref[...] = jnp.sin(ang)


def _general_tables(pos_ref, batch_row, cos_ref, sin_ref):
    f32 = jnp.float32
    n_tok = pos_ref.shape[1]
    half = HEAD_DIM // 2
    freq_idx = jax.lax.broadcasted_iota(jnp.int32, (half, LANES), 0).astype(f32)
    inv_freq = jnp.exp(freq_idx * (-math.log(ROPE_THETA) / half))
    inv_freq = jnp.concatenate([inv_freq] * (n_tok // LANES), axis=1)
    pos = pos_ref[pl.ds(batch_row, 1), :].astype(f32)
    ang = inv_freq * pos
    cos_a, sin_a = jnp.cos(ang), jnp.sin(ang)
    cos_ref[...] = jnp.concatenate([cos_a, cos_a], axis=0).T
    sin_ref[...] = jnp.concatenate([-sin_a, sin_a], axis=0).T


def _is_contiguous(pos_ref, batch_row):
    pos = pos_ref[pl.ds(batch_row, 1), :]
    offset = jax.lax.broadcasted_iota(jnp.int32, pos.shape, 1)
    return jnp.min((pos - pos[:, 0:1] == offset).astype(jnp.int32))


def _contiguous_tables(pos_ref, batch_row, cos_off_ref, sin_off_ref):
    half = HEAD_DIM // 2
    lane = jax.lax.broadcasted_iota(jnp.int32, (1, HEAD_DIM), 1)
    first = pos_ref[pl.ds(batch_row, 1), :][:, 0:1].astype(jnp.float32)
    base = first * _lane_inv_freq()
    cos_b, sin_b = jnp.cos(base), jnp.sin(base)
    sign = jnp.where(lane < half, -1.0, 1.0)
    cos_o, sin_o = cos_off_ref[...], sin_off_ref[...]
    cos_t = cos_b * cos_o - sin_b * sin_o
    sin_t = (sign * sin_b) * cos_o + (sign * cos_b) * sin_o
    return cos_t, sin_t


def _layer_kernel(layer, x_ref, pos_ref, posn_ref, w_in_all, w_out_all, pool_w_all,
                  pool_scale_ref, ret_g_ref, pre_g_ref, post_g_ref, o_ref,
                  w_in_s, w_out_s, pool_f32_ref, proj_ref, cos_off_ref, sin_off_ref,
                  cos_gen_ref, sin_gen_ref, cos_ref, sin_ref, y_ref,
                  state_ref, uhist_ref, contig_ref, sems):
    f32, bf16 = jnp.float32, jnp.bfloat16
    w_in_hbm, w_out_hbm, pool_w_hbm = w_in_all.at[layer], w_out_all.at[layer], pool_w_all.at[layer]
    b, tile = pl.program_id(0), pl.program_id(1)
    n_tiles = pl.num_programs(1)
    n_tok, d_model = x_ref.shape
    group_w = RET_HEADS * HEAD_DIM
    half = HEAD_DIM // 2

    @pl.when((b == 0) & (tile == 0))
    def _():
        _stage_weights(w_in_hbm, w_out_hbm, pool_w_hbm, w_in_s, w_out_s, pool_f32_ref, proj_ref, sems)
        _offset_tables(cos_off_ref, sin_off_ref)
        cos_gen_ref[...] = jnp.zeros_like(cos_gen_ref)
        sin_gen_ref[...] = jnp.zeros_like(sin_gen_ref)
        contig_ref[0] = _is_contiguous(pos_ref, b)

    @pl.when(tile == 0)
    def _():
        state_ref[...] = jnp.zeros_like(state_ref)
        uhist_ref[...] = jnp.zeros_like(uhist_ref)

    contiguous = contig_ref[0]

    @pl.when(contiguous == 0)
    def _():
        _general_tables(pos_ref, b, cos_gen_ref, sin_gen_ref)

    wraps = tile + 1 == n_tiles
    next_row = jnp.minimum(b + wraps.astype(jnp.int32), pl.num_programs(0) - 1)
    contig_ref[0] = _is_contiguous(posn_ref, next_row)

    x = x_ref[...]
    ms = jnp.mean(x * x, axis=-1, keepdims=True)
    hb = (x * jax.lax.rsqrt(ms + EPS) * pre_g_ref[...]).astype(bf16)
    def project(segments):
        for seg in segments:
            cols = slice(seg * group_w, (seg + 1) * group_w)
            proj_ref[:, cols] = jnp.dot(hb, w_in_s[:, cols], preferred_element_type=f32)

    project(range(4))

    use_fast = jnp.broadcast_to(contiguous, (n_tok, HEAD_DIM)) != 0
    cos_fast, sin_fast = _contiguous_tables(pos_ref, b, cos_off_ref, sin_off_ref)
    cos_ref[...] = jnp.where(use_fast, cos_fast, cos_gen_ref[...])
    sin_ref[...] = jnp.where(use_fast, sin_fast, sin_gen_ref[...])

    row = jax.lax.broadcasted_iota(jnp.int32, (RET_CHUNK, HEAD_DIM), 0)
    col = jax.lax.broadcasted_iota(jnp.int32, (RET_CHUNK, HEAD_DIM), 1)
    causal = row >= col
    row_p1 = (row + 1).astype(f32)
    k_scale = HEAD_DIM ** -0.5
    for hd in range(RET_HEADS):
        log_g = math.log(1.0 - 2.0 ** (-5.0 - hd))
        q_decay = jnp.exp(row_p1 * log_g)
        k_decay = jnp.exp(row_p1 * (-log_g)) * k_scale
        chunk_decay = math.exp(log_g * RET_CHUNK)
        hcols = slice(hd * HEAD_DIM, (hd + 1) * HEAD_DIM)
        kcols = slice(group_w + hd * HEAD_DIM, group_w + (hd + 1) * HEAD_DIM)
        vcols = slice(2 * group_w + hd * HEAD_DIM, 2 * group_w + (hd + 1) * HEAD_DIM)
        gcols = slice(3 * group_w + hd * HEAD_DIM, 3 * group_w + (hd + 1) * HEAD_DIM)
        norm_g = ret_g_ref[:, hcols]
        for c in range(n_tok // RET_CHUNK):
            rows = slice(c * RET_CHUNK, (c + 1) * RET_CHUNK)
            cos_c, sin_c = cos_ref[rows, :], sin_ref[rows, :]
            q = proj_ref[rows, hcols]
            k = proj_ref[rows, kcols]
            vb = proj_ref[rows, vcols].astype(bf16)
            qd = ((q * cos_c + pltpu.roll(q, half, axis=1) * sin_c) * q_decay).astype(bf16)
            kd = (k * cos_c + pltpu.roll(k, half, axis=1) * sin_c) * k_decay
            kdt = kd.T.astype(bf16)
            scores = jnp.dot(qd, kdt, preferred_element_type=f32)
            p = jnp.where(causal, scores, 0.0).astype(bf16)
            s_prev = state_ref[hd]
            lhs = jnp.concatenate([p, qd], axis=1)
            rhs = jnp.concatenate([vb, s_prev.astype(bf16)], axis=0)
            o = jnp.dot(lhs, rhs, preferred_element_type=f32)
            kv = jnp.dot(kdt, vb, preferred_element_type=f32)
            state_ref[hd] = (s_prev + kv) * chunk_decay
            mu = jnp.mean(o, axis=-1, keepdims=True)
            dev = o - mu
            var = jnp.mean(dev * dev, axis=-1, keepdims=True)
            gate = _silu(proj_ref[rows, gcols])
            y_ref[rows, hcols] = (dev * jax.lax.rsqrt(var + EPS) * norm_g * gate).astype(bf16)

    project(range(4, N_PROJ))

    u_off, gp_off = 4 * group_w, 5 * group_w
    pool_c = group_w // len(POOL_WINDOWS)
    hist_row = jax.lax.broadcasted_iota(jnp.int32, (POOL_HIST, pool_c), 0)
    t_head = (tile * n_tok + hist_row + 1).astype(f32)
    for g, w in enumerate(POOL_WINDOWS):
        ucols = slice(u_off + g * pool_c, u_off + (g + 1) * pool_c)
        gcols = slice(gp_off + g * pool_c, gp_off + (g + 1) * pool_c)
        pcols = slice(g * pool_c, (g + 1) * pool_c)
        u = proj_ref[:, ucols]
        s = jnp.concatenate([uhist_ref[:, pcols], u], axis=0)
        shift = 1
        while shift < w:
            s = s + pltpu.roll(s, shift, axis=0)
            shift *= 2
        wsum = s[POOL_HIST:]
        inv_head = 1.0 / jnp.minimum(t_head, float(w))
        mean = jnp.concatenate([wsum[:POOL_HIST] * inv_head, wsum[POOL_HIST:] * (1.0 / w)], axis=0)
        gate = _silu(proj_ref[:, gcols])
        y_ref[:, group_w + g * pool_c:group_w + (g + 1) * pool_c] = (
            (mean - u) * pool_scale_ref[:, pcols] * gate).astype(bf16)
    uhist_ref[...] = proj_ref[n_tok - POOL_HIST:, u_off:u_off + group_w]

    for r0 in range(0, n_tok, OUT_ROWS):
        rows = slice(r0, r0 + OUT_ROWS)
        out = jnp.dot(y_ref[rows, :], w_out_s[...], preferred_element_type=f32)
        ms_o = jnp.mean(out * out, axis=-1, keepdims=True)
        o_ref[rows, :] = x_ref[rows, :] + out * jax.lax.rsqrt(ms_o + EPS) * post_g_ref[...]


def _vmem_limit_bytes(d_model, d_proj, d_mix, pool_rows, pool_c, batch):
    scratch = (2 * (d_model * d_proj + d_mix * d_model) + 4 * pool_rows * pool_c
               + 4 * SEQ_TILE * d_proj
               + 2 * SEQ_TILE * d_mix
               + 6 * 4 * SEQ_TILE * HEAD_DIM
               + 4 * RET_HEADS * HEAD_DIM * HEAD_DIM + 4 * POOL_HIST * (d_mix // 2))
    blocks = 2 * (2 * 4 * SEQ_TILE * d_model + 2 * 4 * max(batch, 8) * SEQ_TILE)
    temporaries = 4 * SEQ_TILE * d_proj
    return scratch + blocks + temporaries


def _hybrid_layer(layer, x, positions, w_in, w_out, pool_w, pool_scale, ret_g, pre_g, post_g):
    batch, seq, d_model = x.shape
    d_proj = w_in.shape[2]
    d_mix = w_out.shape[1]
    group_w = d_mix // 2
    _, pool_rows, pool_c = pool_w.shape
    n_tiles = seq // SEQ_TILE
    assert seq % SEQ_TILE == 0 and SEQ_TILE % RET_CHUNK == 0 and SEQ_TILE % (2 * CAST_ROWS) == 0
    assert SEQ_TILE % OUT_ROWS == 0 and SEQ_TILE % LANES == 0
    assert group_w == RET_HEADS * HEAD_DIM and d_proj == N_PROJ * group_w
    assert pool_rows == group_w and pool_rows % pool_c == 0
    assert d_mix % SEQ_TILE == 0 and d_model % (SEQ_TILE // 2) == 0
    assert (d_mix // SEQ_TILE) * d_model <= d_proj and d_mix // SEQ_TILE + 1 <= N_DMA_SEMS

    def pos_next_map(b, t):
        step = jnp.minimum(b * n_tiles + t + 1, batch * n_tiles - 1)
        return 0, step % n_tiles

    hbm = pl.BlockSpec(memory_space=pltpu.HBM)
    row_spec = pl.BlockSpec((1, group_w), lambda b, t: (0, 0))
    table = pltpu.VMEM((SEQ_TILE, HEAD_DIM), jnp.float32)
    return pl.pallas_call(
        functools.partial(_layer_kernel, layer),
        grid=(batch, n_tiles),
        in_specs=[
            pl.BlockSpec((None, SEQ_TILE, d_model), lambda b, t: (b, t, 0)),
            pl.BlockSpec((batch, SEQ_TILE), lambda b, t: (0, t)),
            pl.BlockSpec((batch, SEQ_TILE), pos_next_map),
            hbm, hbm, hbm,
            row_spec, row_spec,
            pl.BlockSpec((1, d_model), lambda b, t: (0, 0)),
            pl.BlockSpec((1, d_model), lambda b, t: (0, 0)),
        ],
        out_specs=pl.BlockSpec((None, SEQ_TILE, d_model), lambda b, t: (b, t, 0)),
        out_shape=jax.ShapeDtypeStruct(x.shape, x.dtype),
        scratch_shapes=[
            pltpu.VMEM((d_model, d_proj), jnp.bfloat16),
            pltpu.VMEM((d_mix, d_model), jnp.bfloat16),
            pltpu.VMEM((pool_rows, pool_c), jnp.float32),
            pltpu.VMEM((SEQ_TILE, d_proj), jnp.float32),
            table, table,
            table, table,
            table, table,
            pltpu.VMEM((SEQ_TILE, d_mix), jnp.bfloat16),
            pltpu.VMEM((RET_HEADS, HEAD_DIM, HEAD_DIM), jnp.float32),
            pltpu.VMEM((POOL_HIST, group_w), jnp.float32),
            pltpu.SMEM((1,), jnp.int32),
            pltpu.SemaphoreType.DMA((N_DMA_SEMS,)),
        ],
        compiler_params=pltpu.CompilerParams(
            dimension_semantics=("arbitrary", "arbitrary"),
            vmem_limit_bytes=_vmem_limit_bytes(d_model, d_proj, d_mix, pool_rows, pool_c, batch)),
        name="hybrid_layer",
    )(x, positions, positions, w_in, w_out, pool_w, pool_scale, ret_g, pre_g, post_g)


@jax.jit
def kernel(x, positions, w_in, w_out, pool_w, pool_scale, ret_norm_g, pre_norm_g, post_norm_g):
    depth = w_in.shape[0]
    pool_w2d = pool_w.reshape(depth, -1, pool_w.shape[-1])
    for layer in range(depth):
        x = _hybrid_layer(
            layer, x, positions, w_in, w_out, pool_w2d,
            pool_scale[layer][None, :], ret_norm_g[layer][None, :],
            pre_norm_g[layer][None, :], post_norm_g[layer][None, :])
    return x
```

```python
import functools
import math

import jax
import jax.numpy as jnp
from jax.experimental import pallas as pl
from jax.experimental.pallas import tpu as pltpu

RET_HEADS = 8
HEAD_DIM = 128
ROPE_THETA = 10000.0
POOL_WINDOWS = (2, 4, 8, 16)
N_PROJ = 6
EPS = 1e-6

SEQ_TILE = 512
RET_CHUNK = 128
OUT_ROWS = 256
POOL_HIST = 16
CAST_ROWS = 16
LANES = 128
N_GAINS = 4
N_DMA_SEMS = 6 + N_GAINS


def _silu(g):
    return g * (1.0 / (1.0 + jnp.exp(-g)))


def _cast_rows(src_ref, src_row0, src_cols, dst_ref, dst_row0, n_rows):
    def body(i, carry):
        r = pl.multiple_of(i * CAST_ROWS, CAST_ROWS)
        dst_ref[pl.ds(dst_row0 + r, CAST_ROWS), :] = (
            src_ref[pl.ds(src_row0 + r, CAST_ROWS), src_cols].astype(jnp.bfloat16))
        return carry
    jax.lax.fori_loop(0, n_rows // CAST_ROWS, body, 0)


def _stage_weights(w_in_hbm, w_out_hbm, pool_w_hbm, gains_hbm, w_in_s, w_out_s, pool_w_s, gains_s,
                   stage_ref, sems):
    n_tok = stage_ref.shape[0]
    d_model = w_out_hbm.shape[1]
    pool_c = pool_w_hbm.shape[1]
    n_out_chunks = w_out_hbm.shape[0] // n_tok
    pool_rows = pool_w_hbm.shape[0]
    n_pool_chunks = pool_rows // n_tok
    pool_col0 = n_out_chunks * d_model
    n_small = n_out_chunks + n_pool_chunks

    def small_copy(i):
        if i < n_out_chunks:
            src = w_out_hbm.at[pl.ds(i * n_tok, n_tok), :]
            dst = stage_ref.at[:, pl.ds(i * d_model, d_model)]
        elif i < n_small:
            j = i - n_out_chunks
            src = pool_w_hbm.at[pl.ds(j * n_tok, n_tok), :]
            dst = stage_ref.at[:, pl.ds(pool_col0 + j * pool_c, pool_c)]
        else:
            src, dst = gains_hbm[i - n_small], gains_s[i - n_small]
        return pltpu.make_async_copy(src, dst, sems.at[i])

    for i in range(n_small + N_GAINS):
        small_copy(i).start()
    for i in range(n_small + N_GAINS):
        small_copy(i).wait()
    for i in range(n_out_chunks):
        _cast_rows(stage_ref, 0, slice(i * d_model, (i + 1) * d_model), w_out_s, i * n_tok, n_tok)
    for j in range(n_pool_chunks):
        _cast_rows(stage_ref, 0, slice(pool_col0 + j * pool_c, pool_col0 + (j + 1) * pool_c),
                   pool_w_s, j * n_tok, n_tok)

    half = n_tok // 2
    n_in_chunks = w_in_hbm.shape[0] // half

    def in_copy(c):
        slot = c % 2
        return pltpu.make_async_copy(w_in_hbm.at[pl.ds(c * half, half), :],
                                     stage_ref.at[pl.ds(slot * half, half), :], sems.at[slot])

    in_copy(0).start()
    in_copy(1).start()
    for c in range(n_in_chunks):
        in_copy(c).wait()
        _cast_rows(stage_ref, (c % 2) * half, slice(None), w_in_s, c * half, half)
        if c + 2 < n_in_chunks:
            in_copy(c + 2).start()


def _pre_norm(x_ref, pre_g_ref, hb_ref):
    x = x_ref[...]
    ms = jnp.mean(x * x, axis=-1, keepdims=True)
    hb_ref[...] = (x * jax.lax.rsqrt(ms + EPS) * pre_g_ref[...]).astype(jnp.bfloat16)


def _lane_inv_freq():
    half = HEAD_DIM // 2
    lane = jax.lax.broadcasted_iota(jnp.int32, (1, HEAD_DIM), 1)
    freq_idx = (lane & (half - 1)).astype(jnp.float32)
    return jnp.exp(freq_idx * (-math.log(ROPE_THETA) / half))


def _offset_tables(cos_off_ref, sin_off_ref):
    n_tok = cos_off_ref.shape[0]
    offset = jax.lax.broadcasted_iota(jnp.int32, (n_tok, HEAD_DIM), 0).astype(jnp.float32)
    ang = offset * _lane_inv_freq()
    cos_off_ref[...] = jnp.cos(ang)
    sin_off_ref[...] = jnp.sin(ang)


def _general_tables(pos_ref, batch_row, cos_ref, sin_ref):
    f32 = jnp.float32
    n_tok = pos_ref.shape[1]
    half = HEAD_DIM // 2
    freq_idx = jax.lax.broadcasted_iota(jnp.int32, (half, LANES), 0).astype(f32)
    inv_freq = jnp.exp(freq_idx * (-math.log(ROPE_THETA) / half))
    inv_freq = jnp.concatenate([inv_freq] * (n_tok // LANES), axis=1)
    pos = pos_ref[pl.ds(batch_row, 1), :].astype(f32)
    ang = inv_freq * pos
    cos_a, sin_a = jnp.cos(ang), jnp.sin(ang)
    cos_ref[...] = jnp.concatenate([cos_a, cos_a], axis=0).T
    sin_ref[...] = jnp.concatenate([-sin_a, sin_a], axis=0).T


def _is_contiguous(pos_ref, batch_row):
    pos = pos_ref[pl.ds(batch_row, 1), :]
    offset = jax.lax.broadcasted_iota(jnp.int32, pos.shape, 1)
    return jnp.min((pos - pos[:, 0:1] == offset).astype(jnp.int32))


def _contiguous_tables(pos_ref, batch_row, cos_off_ref, sin_off_ref):
    half = HEAD_DIM // 2
    lane = jax.lax.broadcasted_iota(jnp.int32, (1, HEAD_DIM), 1)
    first = pos_ref[pl.ds(batch_row, 1), :][:, 0:1].astype(jnp.float32)
    base = first * _lane_inv_freq()
    cos_b, sin_b = jnp.cos(base), jnp.sin(base)
    sign = jnp.where(lane < half, -1.0, 1.0)
    cos_o, sin_o = cos_off_ref[...], sin_off_ref[...]
    cos_t = cos_b * cos_o - sin_b * sin_o
    sin_t = (sign * sin_b) * cos_o + (sign * cos_b) * sin_o
    return cos_t, sin_t


def _layer_kernel(layer, x_ref, xn_ref, pos_ref, posn_ref, w_in_all, w_out_all, pool_w_all,
                  pool_scale_hbm, ret_g_hbm, pre_g_hbm, post_g_hbm, o_ref,
                  w_in_s, w_out_s, pool_w_s, pool_scale_ref, ret_g_ref, pre_g_ref, post_g_ref,
                  proj_ref, hb_ref, cos_off_ref, sin_off_ref,
                  cos_gen_ref, sin_gen_ref, cos_ref, sin_ref, y_ref,
                  state_ref, uhist_ref, contig_ref, sems):
    f32, bf16 = jnp.float32, jnp.bfloat16
    w_in_hbm, w_out_hbm, pool_w_hbm = w_in_all.at[layer], w_out_all.at[layer], pool_w_all.at[layer]
    b, tile = pl.program_id(0), pl.program_id(1)
    n_tiles = pl.num_programs(1)
    n_tok, d_model = x_ref.shape
    group_w = RET_HEADS * HEAD_DIM
    half = HEAD_DIM // 2

    @pl.when((b == 0) & (tile == 0))
    def _():
        gains_hbm = [g.at[pl.ds(layer, 1), :] for g in (pool_scale_hbm, ret_g_hbm, pre_g_hbm, post_g_hbm)]
        gains_s = [pool_scale_ref, ret_g_ref, pre_g_ref, post_g_ref]
        _stage_weights(w_in_hbm, w_out_hbm, pool_w_hbm, gains_hbm, w_in_s, w_out_s, pool_w_s, gains_s,
                       proj_ref, sems)
        _offset_tables(cos_off_ref, sin_off_ref)
        cos_gen_ref[...] = jnp.zeros_like(cos_gen_ref)
        sin_gen_ref[...] = jnp.zeros_like(sin_gen_ref)
        contig_ref[0] = _is_contiguous(pos_ref, b)
        _pre_norm(x_ref, pre_g_ref, hb_ref)

    @pl.when(tile == 0)
    def _():
        state_ref[...] = jnp.zeros_like(state_ref)
        uhist_ref[...] = jnp.zeros_like(uhist_ref)

    contiguous = contig_ref[0]

    @pl.when(contiguous == 0)
    def _():
        _general_tables(pos_ref, b, cos_gen_ref, sin_gen_ref)

    wraps = tile + 1 == n_tiles
    next_row = jnp.minimum(b + wraps.astype(jnp.int32), pl.num_programs(0) - 1)
    contig_ref[0] = _is_contiguous(posn_ref, next_row)

    def project(first_seg, last_seg):
        cols = slice(first_seg * group_w, last_seg * group_w)
        proj_ref[:, cols] = jnp.dot(hb_ref[...], w_in_s[:, cols], preferred_element_type=f32)

    for seg in range(4):
        project(seg, seg + 1)

    use_fast = jnp.broadcast_to(contiguous, (n_tok, HEAD_DIM)) != 0
    cos_fast, sin_fast = _contiguous_tables(pos_ref, b, cos_off_ref, sin_off_ref)
    cos_ref[...] = jnp.where(use_fast, cos_fast, cos_gen_ref[...])
    sin_ref[...] = jnp.where(use_fast, sin_fast, sin_gen_ref[...])

    row = jax.lax.broadcasted_iota(jnp.int32, (RET_CHUNK, HEAD_DIM), 0)
    col = jax.lax.broadcasted_iota(jnp.int32, (RET_CHUNK, HEAD_DIM), 1)
    causal = row >= col
    row_p1 = (row + 1).astype(f32)
    k_scale = HEAD_DIM ** -0.5
    for hd in range(RET_HEADS):
        log_g = math.log(1.0 - 2.0 ** (-5.0 - hd))
        q_decay = jnp.exp(row_p1 * log_g)
        k_decay = jnp.exp(row_p1 * (-log_g)) * k_scale
        chunk_decay = math.exp(log_g * RET_CHUNK)
        hcols = slice(hd * HEAD_DIM, (hd + 1) * HEAD_DIM)
        kcols = slice(group_w + hd * HEAD_DIM, group_w + (hd + 1) * HEAD_DIM)
        vcols = slice(2 * group_w + hd * HEAD_DIM, 2 * group_w + (hd + 1) * HEAD_DIM)
        gcols = slice(3 * group_w + hd * HEAD_DIM, 3 * group_w + (hd + 1) * HEAD_DIM)
        norm_g = ret_g_ref[:, hcols]
        for c in range(n_tok // RET_CHUNK):
            rows = slice(c * RET_CHUNK, (c + 1) * RET_CHUNK)
            cos_c, sin_c = cos_ref[rows, :], sin_ref[rows, :]
            q = proj_ref[rows, hcols]
            k = proj_ref[rows, kcols]
            vb = proj_ref[rows, vcols].astype(bf16)
            qd = ((q * cos_c + pltpu.roll(q, half, axis=1) * sin_c) * q_decay).astype(bf16)
            kd = (k * cos_c + pltpu.roll(k, half, axis=1) * sin_c) * k_decay
            kdt = kd.T.astype(bf16)
            scores = jnp.dot(qd, kdt, preferred_element_type=f32)
            p = jnp.where(causal, scores, 0.0).astype(bf16)
            s_prev = state_ref[hd]
            lhs = jnp.concatenate([p, qd], axis=1)
            rhs = jnp.concatenate([vb, s_prev.astype(bf16)], axis=0)
            o = jnp.dot(lhs, rhs, preferred_element_type=f32)
            kv = jnp.dot(kdt, vb, preferred_element_type=f32)
            state_ref[hd] = (s_prev + kv) * chunk_decay
            mu = jnp.mean(o, axis=-1, keepdims=True)
            dev = o - mu
            var = jnp.mean(dev * dev, axis=-1, keepdims=True)
            gate = _silu(proj_ref[rows, gcols])
            y_ref[rows, hcols] = (dev * jax.lax.rsqrt(var + EPS) * norm_g * gate).astype(bf16)

    project(4, N_PROJ)

    u_off, gp_off = 4 * group_w, 5 * group_w
    pool_c = group_w // len(POOL_WINDOWS)
    hist_row = jax.lax.broadcasted_iota(jnp.int32, (POOL_HIST, pool_c), 0)
    t_head = (tile * n_tok + hist_row + 1).astype(f32)
    for g, w in enumerate(POOL_WINDOWS):
        ucols = slice(u_off + g * pool_c, u_off + (g + 1) * pool_c)
        gcols = slice(gp_off + g * pool_c, gp_off + (g + 1) * pool_c)
        pcols = slice(g * pool_c, (g + 1) * pool_c)
        u = proj_ref[:, ucols]
        s = jnp.concatenate([uhist_ref[:, pcols], u], axis=0)
        shift = 1
        while shift < w:
            s = s + pltpu.roll(s, shift, axis=0)
            shift *= 2
        wsum = s[POOL_HIST:]
        inv_head = 1.0 / jnp.minimum(t_head, float(w))
        mean = jnp.concatenate([wsum[:POOL_HIST] * inv_head, wsum[POOL_HIST:] * (1.0 / w)], axis=0)
        mixed = (mean - u).astype(bf16)
        pm = jnp.dot(mixed, pool_w_s[g * pool_c:(g + 1) * pool_c, :], preferred_element_type=f32)
        gate = _silu(proj_ref[:, gcols])
        y_ref[:, group_w + g * pool_c:group_w + (g + 1) * pool_c] = (
            pm * pool_scale_ref[:, pcols] * gate).astype(bf16)
    uhist_ref[...] = proj_ref[n_tok - POOL_HIST:, u_off:u_off + group_w]

    for r0 in range(0, n_tok, OUT_ROWS):
        rows = slice(r0, r0 + OUT_ROWS)
        out = jnp.dot(y_ref[rows, :], w_out_s[...], preferred_element_type=f32)
        ms_o = jnp.mean(out * out, axis=-1, keepdims=True)
        o_ref[rows, :] = x_ref[rows, :] + out * jax.lax.rsqrt(ms_o + EPS) * post_g_ref[...]

    _pre_norm(xn_ref, pre_g_ref, hb_ref)


def _vmem_limit_bytes(d_model, d_proj, d_mix, pool_rows, pool_c, batch):
    scratch = (2 * (d_model * d_proj + d_mix * d_model + pool_rows * pool_c)
               + 4 * SEQ_TILE * d_proj
               + 2 * SEQ_TILE * (d_mix + d_model)
               + 6 * 4 * SEQ_TILE * HEAD_DIM
               + 4 * RET_HEADS * HEAD_DIM * HEAD_DIM + 4 * POOL_HIST * (d_mix // 2))
    blocks = 2 * (3 * 4 * SEQ_TILE * d_model + 2 * 4 * max(batch, 8) * SEQ_TILE)
    temporaries = 4 * SEQ_TILE * d_proj
    return scratch + blocks + temporaries


def _hybrid_layer(layer, x, positions, w_in, w_out, pool_w, pool_scale, ret_g, pre_g, post_g):
    batch, seq, d_model = x.shape
    d_proj = w_in.shape[2]
    d_mix = w_out.shape[1]
    group_w = d_mix // 2
    _, pool_rows, pool_c = pool_w.shape
    n_tiles = seq // SEQ_TILE
    assert seq % SEQ_TILE == 0 and SEQ_TILE % RET_CHUNK == 0 and SEQ_TILE % (2 * CAST_ROWS) == 0
    assert SEQ_TILE % OUT_ROWS == 0 and SEQ_TILE % LANES == 0
    assert group_w == RET_HEADS * HEAD_DIM and d_proj == N_PROJ * group_w
    assert d_mix % SEQ_TILE == 0 and pool_rows % SEQ_TILE == 0 and d_model % (SEQ_TILE // 2) == 0
    n_small = d_mix // SEQ_TILE + pool_rows // SEQ_TILE
    assert (d_mix // SEQ_TILE) * d_model + (pool_rows // SEQ_TILE) * pool_c <= d_proj
    assert n_small + N_GAINS <= N_DMA_SEMS

    def next_tile(b, t):
        step = jnp.minimum(b * n_tiles + t + 1, batch * n_tiles - 1)
        return step // n_tiles, step % n_tiles

    def x_next_map(b, t):
        nb, nt = next_tile(b, t)
        return nb, nt, 0

    def pos_next_map(b, t):
        return 0, next_tile(b, t)[1]

    hbm = pl.BlockSpec(memory_space=pltpu.HBM)
    table = pltpu.VMEM((SEQ_TILE, HEAD_DIM), jnp.float32)
    return pl.pallas_call(
        functools.partial(_layer_kernel, layer),
        grid=(batch, n_tiles),
        in_specs=[
            pl.BlockSpec((None, SEQ_TILE, d_model), lambda b, t: (b, t, 0)),
            pl.BlockSpec((None, SEQ_TILE, d_model), x_next_map),
            pl.BlockSpec((batch, SEQ_TILE), lambda b, t: (0, t)),
            pl.BlockSpec((batch, SEQ_TILE), pos_next_map),
            hbm, hbm, hbm,
            hbm, hbm, hbm, hbm,
        ],
        out_specs=pl.BlockSpec((None, SEQ_TILE, d_model), lambda b, t: (b, t, 0)),
        out_shape=jax.ShapeDtypeStruct(x.shape, x.dtype),
        scratch_shapes=[
            pltpu.VMEM((d_model, d_proj), jnp.bfloat16),
            pltpu.VMEM((d_mix, d_model), jnp.bfloat16),
            pltpu.VMEM((pool_rows, pool_c), jnp.bfloat16),
            pltpu.VMEM((1, group_w), jnp.float32),
            pltpu.VMEM((1, group_w), jnp.float32),
            pltpu.VMEM((1, d_model), jnp.float32),
            pltpu.VMEM((1, d_model), jnp.float32),
            pltpu.VMEM((SEQ_TILE, d_proj), jnp.float32),
            pltpu.VMEM((SEQ_TILE, d_model), jnp.bfloat16),
            table, table,
            table, table,
            table, table,
            pltpu.VMEM((SEQ_TILE, d_mix), jnp.bfloat16),
            pltpu.VMEM((RET_HEADS, HEAD_DIM, HEAD_DIM), jnp.float32),
            pltpu.VMEM((POOL_HIST, group_w), jnp.float32),
            pltpu.SMEM((1,), jnp.int32),
            pltpu.SemaphoreType.DMA((N_DMA_SEMS,)),
        ],
        compiler_params=pltpu.CompilerParams(
            dimension_semantics=("arbitrary", "arbitrary"),
            vmem_limit_bytes=_vmem_limit_bytes(d_model, d_proj, d_mix, pool_rows, pool_c, batch)),
        name="hybrid_layer",
    )(x, x, positions, positions, w_in, w_out, pool_w, pool_scale, ret_g, pre_g, post_g)


@jax.jit
def kernel(x, positions, w_in, w_out, pool_w, pool_scale, ret_norm_g, pre_norm_g, post_norm_g):
    depth = w_in.shape[0]
    pool_w2d = pool_w.reshape(depth, -1, pool_w.shape[-1])
    for layer in range(depth):
        x = _hybrid_layer(layer, x, positions, w_in, w_out, pool_w2d,
                          pool_scale, ret_norm_g, pre_norm_g, post_norm_g)
    return x
```

```python
import functools
import math

import jax
import jax.numpy as jnp
from jax.experimental import pallas as pl
from jax.experimental.pallas import tpu as pltpu

RET_HEADS = 8
HEAD_DIM = 128
ROPE_THETA = 10000.0
POOL_WINDOWS = (2, 4, 8, 16)
N_PROJ = 6
EPS = 1e-6

SEQ_TILE = 512
SUBS_PER_STEP = 2
RET_CHUNK = 128
OUT_ROWS = 256
POOL_HIST = 16
CAST_ROWS = 16
LANES = 128
N_GAINS = 4
N_DMA_SEMS = 6 + N_GAINS


def _silu(g):
    return g * (1.0 / (1.0 + jnp.exp(-g)))


def _cast_rows(src_ref, src_row0, src_cols, dst_ref, dst_row0, n_rows):
    def body(i, carry):
        r = pl.multiple_of(i * CAST_ROWS, CAST_ROWS)
        dst_ref[pl.ds(dst_row0 + r, CAST_ROWS), :] = (
            src_ref[pl.ds(src_row0 + r, CAST_ROWS), src_cols].astype(jnp.bfloat16))
        return carry
    jax.lax.fori_loop(0, n_rows // CAST_ROWS, body, 0)


def _stage_weights(w_in_hbm, w_out_hbm, pool_w_hbm, gains_hbm, w_in_s, w_out_s, pool_w_s, gains_s,
                   stage_ref, sems):
    n_tok = stage_ref.shape[0]
    d_model = w_out_hbm.shape[1]
    pool_c = pool_w_hbm.shape[1]
    n_out_chunks = w_out_hbm.shape[0] // n_tok
    pool_rows = pool_w_hbm.shape[0]
    n_pool_chunks = pool_rows // n_tok
    pool_col0 = n_out_chunks * d_model
    n_small = n_out_chunks + n_pool_chunks

    def small_copy(i):
        if i < n_out_chunks:
            src = w_out_hbm.at[pl.ds(i * n_tok, n_tok), :]
            dst = stage_ref.at[:, pl.ds(i * d_model, d_model)]
        elif i < n_small:
            j = i - n_out_chunks
            src = pool_w_hbm.at[pl.ds(j * n_tok, n_tok), :]
            dst = stage_ref.at[:, pl.ds(pool_col0 + j * pool_c, pool_c)]
        else:
            src, dst = gains_hbm[i - n_small], gains_s[i - n_small]
        return pltpu.make_async_copy(src, dst, sems.at[i])

    for i in range(n_small + N_GAINS):
        small_copy(i).start()
    for i in range(n_small + N_GAINS):
        small_copy(i).wait()
    for i in range(n_out_chunks):
        _cast_rows(stage_ref, 0, slice(i * d_model, (i + 1) * d_model), w_out_s, i * n_tok, n_tok)
    for j in range(n_pool_chunks):
        _cast_rows(stage_ref, 0, slice(pool_col0 + j * pool_c, pool_col0 + (j + 1) * pool_c),
                   pool_w_s, j * n_tok, n_tok)

    half = n_tok // 2
    n_in_chunks = w_in_hbm.shape[0] // half

    def in_copy(c):
        slot = c % 2
        return pltpu.make_async_copy(w_in_hbm.at[pl.ds(c * half, half), :],
                                     stage_ref.at[pl.ds(slot * half, half), :], sems.at[slot])

    in_copy(0).start()
    in_copy(1).start()
    for c in range(n_in_chunks):
        in_copy(c).wait()
        _cast_rows(stage_ref, (c % 2) * half, slice(None), w_in_s, c * half, half)
        if c + 2 < n_in_chunks:
            in_copy(c + 2).start()


def _lane_inv_freq():
    half = HEAD_DIM // 2
    lane = jax.lax.broadcasted_iota(jnp.int32, (1, HEAD_DIM), 1)
    freq_idx = (lane & (half - 1)).astype(jnp.float32)
    return jnp.exp(freq_idx * (-math.log(ROPE_THETA) / half))


def _offset_tables(cos_off_ref, sin_off_ref):
    n_tok = cos_off_ref.shape[0]
    offset = jax.lax.broadcasted_iota(jnp.int32, (n_tok, HEAD_DIM), 0).astype(jnp.float32)
    ang = offset * _lane_inv_freq()
    cos_off_ref[...] = jnp.cos(ang)
    sin_off_ref[...] = jnp.sin(ang)


def _tile_positions(pos_ref, batch_row, sub):
    return pos_ref[pl.ds(batch_row, 1), :][:, sub * SEQ_TILE:(sub + 1) * SEQ_TILE]


def _general_tables(pos, cos_ref, sin_ref):
    f32 = jnp.float32
    half = HEAD_DIM // 2
    freq_idx = jax.lax.broadcasted_iota(jnp.int32, (half, LANES), 0).astype(f32)
    inv_freq = jnp.exp(freq_idx * (-math.log(ROPE_THETA) / half))
    inv_freq = jnp.concatenate([inv_freq] * (pos.shape[1] // LANES), axis=1)
    ang = inv_freq * pos.astype(f32)
    cos_a, sin_a = jnp.cos(ang), jnp.sin(ang)
    cos_ref[...] = jnp.concatenate([cos_a, cos_a], axis=0).T
    sin_ref[...] = jnp.concatenate([-sin_a, sin_a], axis=0).T


def _is_contiguous(pos):
    offset = jax.lax.broadcasted_iota(jnp.int32, pos.shape, 1)
    return jnp.min((pos - pos[:, 0:1] == offset).astype(jnp.int32))


def _contiguous_tables(pos, cos_off_ref, sin_off_ref):
    half = HEAD_DIM // 2
    lane = jax.lax.broadcasted_iota(jnp.int32, (1, HEAD_DIM), 1)
    base = pos[:, 0:1].astype(jnp.float32) * _lane_inv_freq()
    cos_b, sin_b = jnp.cos(base), jnp.sin(base)
    sign = jnp.where(lane < half, -1.0, 1.0)
    cos_o, sin_o = cos_off_ref[...], sin_off_ref[...]
    cos_t = cos_b * cos_o - sin_b * sin_o
    sin_t = (sign * sin_b) * cos_o + (sign * cos_b) * sin_o
    return cos_t, sin_t


def _layer_kernel(layer, x_ref, pos_ref, posn_ref, w_in_all, w_out_all, pool_w_all,
                  pool_scale_hbm, ret_g_hbm, pre_g_hbm, post_g_hbm, o_ref,
                  w_in_s, w_out_s, pool_w_s, pool_scale_ref, ret_g_ref, pre_g_ref, post_g_ref,
                  proj_ref, hb_ref, cos_off_ref, sin_off_ref,
                  cos_gen_ref, sin_gen_ref, cos_ref, sin_ref, y_ref,
                  state_ref, uhist_ref, contig_ref, sems):
    f32, bf16 = jnp.float32, jnp.bfloat16
    w_in_hbm, w_out_hbm, pool_w_hbm = w_in_all.at[layer], w_out_all.at[layer], pool_w_all.at[layer]
    b, blk = pl.program_id(0), pl.program_id(1)
    n_blocks = pl.num_programs(1)
    n_tok = SEQ_TILE
    group_w = RET_HEADS * HEAD_DIM
    half = HEAD_DIM // 2

    @pl.when((b == 0) & (blk == 0))
    def _():
        gains_hbm = [g.at[pl.ds(layer, 1), :] for g in (pool_scale_hbm, ret_g_hbm, pre_g_hbm, post_g_hbm)]
        gains_s = [pool_scale_ref, ret_g_ref, pre_g_ref, post_g_ref]
        _stage_weights(w_in_hbm, w_out_hbm, pool_w_hbm, gains_hbm, w_in_s, w_out_s, pool_w_s, gains_s,
                       proj_ref, sems)
        _offset_tables(cos_off_ref, sin_off_ref)
        cos_gen_ref[...] = jnp.zeros_like(cos_gen_ref)
        sin_gen_ref[...] = jnp.zeros_like(sin_gen_ref)
        for sub in range(SUBS_PER_STEP):
            contig_ref[sub] = _is_contiguous(_tile_positions(pos_ref, b, sub))

    @pl.when(blk == 0)
    def _():
        state_ref[...] = jnp.zeros_like(state_ref)
        uhist_ref[...] = jnp.zeros_like(uhist_ref)

    contiguous = [contig_ref[sub] for sub in range(SUBS_PER_STEP)]
    for sub in range(SUBS_PER_STEP):
        @pl.when(contiguous[sub] == 0)
        def _():
            _general_tables(_tile_positions(pos_ref, b, sub), cos_gen_ref.at[sub], sin_gen_ref.at[sub])

    wraps = blk + 1 == n_blocks
    next_row = jnp.minimum(b + wraps.astype(jnp.int32), pl.num_programs(0) - 1)
    for sub in range(SUBS_PER_STEP):
        contig_ref[sub] = _is_contiguous(_tile_positions(posn_ref, next_row, sub))

    row = jax.lax.broadcasted_iota(jnp.int32, (RET_CHUNK, HEAD_DIM), 0)
    col = jax.lax.broadcasted_iota(jnp.int32, (RET_CHUNK, HEAD_DIM), 1)
    causal = row >= col
    row_p1 = (row + 1).astype(f32)
    k_scale = HEAD_DIM ** -0.5
    pool_c = group_w // len(POOL_WINDOWS)
    hist_row = jax.lax.broadcasted_iota(jnp.int32, (POOL_HIST, pool_c), 0)

    for sub in range(SUBS_PER_STEP):
        tok0 = sub * n_tok

        x = x_ref[tok0:tok0 + n_tok, :]
        ms = jnp.mean(x * x, axis=-1, keepdims=True)
        hb_ref[...] = (x * jax.lax.rsqrt(ms + EPS) * pre_g_ref[...]).astype(bf16)

        def project(col0, n_cols):
            cols = slice(col0, col0 + n_cols)
            proj_ref[:, cols] = jnp.dot(hb_ref[...], w_in_s[:, cols], preferred_element_type=f32)

        for seg in range(4):
            project(seg * group_w, group_w)

        use_fast = jnp.broadcast_to(contiguous[sub], (n_tok, HEAD_DIM)) != 0
        cos_fast, sin_fast = _contiguous_tables(_tile_positions(pos_ref, b, sub), cos_off_ref, sin_off_ref)
        cos_ref[...] = jnp.where(use_fast, cos_fast, cos_gen_ref[sub])
        sin_ref[...] = jnp.where(use_fast, sin_fast, sin_gen_ref[sub])

        for hd in range(RET_HEADS):
            log_g = math.log(1.0 - 2.0 ** (-5.0 - hd))
            q_decay = jnp.exp(row_p1 * log_g)
            k_decay = jnp.exp(row_p1 * (-log_g)) * k_scale
            chunk_decay = math.exp(log_g * RET_CHUNK)
            hcols = slice(hd * HEAD_DIM, (hd + 1) * HEAD_DIM)
            kcols = slice(group_w + hd * HEAD_DIM, group_w + (hd + 1) * HEAD_DIM)
            vcols = slice(2 * group_w + hd * HEAD_DIM, 2 * group_w + (hd + 1) * HEAD_DIM)
            gcols = slice(3 * group_w + hd * HEAD_DIM, 3 * group_w + (hd + 1) * HEAD_DIM)
            norm_g = ret_g_ref[:, hcols]
            for c in range(n_tok // RET_CHUNK):
                rows = slice(c * RET_CHUNK, (c + 1) * RET_CHUNK)
                cos_c, sin_c = cos_ref[rows, :], sin_ref[rows, :]
                q = proj_ref[rows, hcols]
                k = proj_ref[rows, kcols]
                vb = proj_ref[rows, vcols].astype(bf16)
                qd = ((q * cos_c + pltpu.roll(q, half, axis=1) * sin_c) * q_decay).astype(bf16)
                kd = (k * cos_c + pltpu.roll(k, half, axis=1) * sin_c) * k_decay
                kdt = kd.T.astype(bf16)
                scores = jnp.dot(qd, kdt, preferred_element_type=f32)
                p = jnp.where(causal, scores, 0.0).astype(bf16)
                s_prev = state_ref[hd]
                lhs = jnp.concatenate([p, qd], axis=1)
                rhs = jnp.concatenate([vb, s_prev.astype(bf16)], axis=0)
                o = jnp.dot(lhs, rhs, preferred_element_type=f32)
                kv = jnp.dot(kdt, vb, preferred_element_type=f32)
                state_ref[hd] = (s_prev + kv) * chunk_decay
                mu = jnp.mean(o, axis=-1, keepdims=True)
                dev = o - mu
                var = jnp.mean(dev * dev, axis=-1, keepdims=True)
                gate = _silu(proj_ref[rows, gcols])
                y_ref[rows, hcols] = (dev * jax.lax.rsqrt(var + EPS) * norm_g * gate).astype(bf16)

        project(4 * group_w, 2 * group_w)

        u_off, gp_off = 4 * group_w, 5 * group_w
        tile_tok0 = (blk * SUBS_PER_STEP + sub) * n_tok
        t_head = (tile_tok0 + hist_row + 1).astype(f32)
        for g, w in enumerate(POOL_WINDOWS):
            ucols = slice(u_off + g * pool_c, u_off + (g + 1) * pool_c)
            gcols = slice(gp_off + g * pool_c, gp_off + (g + 1) * pool_c)
            pcols = slice(g * pool_c, (g + 1) * pool_c)
            u = proj_ref[:, ucols]
            s = jnp.concatenate([uhist_ref[:, pcols], u], axis=0)
            shift = 1
            while shift < w:
                s = s + pltpu.roll(s, shift, axis=0)
                shift *= 2
            wsum = s[POOL_HIST:]
            inv_head = 1.0 / jnp.minimum(t_head, float(w))
            mean = jnp.concatenate([wsum[:POOL_HIST] * inv_head, wsum[POOL_HIST:] * (1.0 / w)], axis=0)
            mixed = (mean - u).astype(bf16)
            pm = jnp.dot(mixed, pool_w_s[g * pool_c:(g + 1) * pool_c, :], preferred_element_type=f32)
            gate = _silu(proj_ref[:, gcols])
            y_ref[:, group_w + g * pool_c:group_w + (g + 1) * pool_c] = (
                pm * pool_scale_ref[:, pcols] * gate).astype(bf16)
        uhist_ref[...] = proj_ref[n_tok - POOL_HIST:, u_off:u_off + group_w]

        for r0 in range(0, n_tok, OUT_ROWS):
            out = jnp.dot(y_ref[r0:r0 + OUT_ROWS, :], w_out_s[...], preferred_element_type=f32)
            ms_o = jnp.mean(out * out, axis=-1, keepdims=True)
            orows = slice(tok0 + r0, tok0 + r0 + OUT_ROWS)
            o_ref[orows, :] = x_ref[orows, :] + out * jax.lax.rsqrt(ms_o + EPS) * post_g_ref[...]


def _vmem_limit_bytes(d_model, d_proj, d_mix, pool_rows, pool_c, batch):
    step_tok = SEQ_TILE * SUBS_PER_STEP
    scratch = (2 * (d_model * d_proj + d_mix * d_model + pool_rows * pool_c)
               + 4 * SEQ_TILE * d_proj
               + 2 * SEQ_TILE * (d_mix + d_model)
               + (4 + 2 * SUBS_PER_STEP) * 4 * SEQ_TILE * HEAD_DIM
               + 4 * RET_HEADS * HEAD_DIM * HEAD_DIM + 4 * POOL_HIST * (d_mix // 2))
    blocks = 2 * (2 * 4 * step_tok * d_model + 2 * 4 * max(batch, 8) * step_tok)
    temporaries = 3 * SEQ_TILE * d_proj
    return scratch + blocks + temporaries


def _hybrid_layer(layer, x, positions, w_in, w_out, pool_w, pool_scale, ret_g, pre_g, post_g):
    batch, seq, d_model = x.shape
    d_proj = w_in.shape[2]
    d_mix = w_out.shape[1]
    group_w = d_mix // 2
    _, pool_rows, pool_c = pool_w.shape
    step_tok = SEQ_TILE * SUBS_PER_STEP
    n_blocks = seq // step_tok
    assert seq % step_tok == 0 and SEQ_TILE % RET_CHUNK == 0 and SEQ_TILE % (2 * CAST_ROWS) == 0
    assert SEQ_TILE % OUT_ROWS == 0 and SEQ_TILE % LANES == 0
    assert group_w == RET_HEADS * HEAD_DIM and d_proj == N_PROJ * group_w
    assert d_mix % SEQ_TILE == 0 and pool_rows % SEQ_TILE == 0 and d_model % (SEQ_TILE // 2) == 0
    n_small = d_mix // SEQ_TILE + pool_rows // SEQ_TILE
    assert (d_mix // SEQ_TILE) * d_model + (pool_rows // SEQ_TILE) * pool_c <= d_proj
    assert n_small + N_GAINS <= N_DMA_SEMS

    def pos_next_map(b, t):
        step = jnp.minimum(b * n_blocks + t + 1, batch * n_blocks - 1)
        return 0, step % n_blocks

    hbm = pl.BlockSpec(memory_space=pltpu.HBM)
    table = pltpu.VMEM((SEQ_TILE, HEAD_DIM), jnp.float32)
    sub_tables = pltpu.VMEM((SUBS_PER_STEP, SEQ_TILE, HEAD_DIM), jnp.float32)
    return pl.pallas_call(
        functools.partial(_layer_kernel, layer),
        grid=(batch, n_blocks),
        in_specs=[
            pl.BlockSpec((None, step_tok, d_model), lambda b, t: (b, t, 0)),
            pl.BlockSpec((batch, step_tok), lambda b, t: (0, t)),
            pl.BlockSpec((batch, step_tok), pos_next_map),
            hbm, hbm, hbm,
            hbm, hbm, hbm, hbm,
        ],
        out_specs=pl.BlockSpec((None, step_tok, d_model), lambda b, t: (b, t, 0)),
        out_shape=jax.ShapeDtypeStruct(x.shape, x.dtype),
        scratch_shapes=[
            pltpu.VMEM((d_model, d_proj), jnp.bfloat16),
            pltpu.VMEM((d_mix, d_model), jnp.bfloat16),
            pltpu.VMEM((pool_rows, pool_c), jnp.bfloat16),
            pltpu.VMEM((1, group_w), jnp.float32),
            pltpu.VMEM((1, group_w), jnp.float32),
            pltpu.VMEM((1, d_model), jnp.float32),
            pltpu.VMEM((1, d_model), jnp.float32),
            pltpu.VMEM((SEQ_TILE, d_proj), jnp.float32),
            pltpu.VMEM((SEQ_TILE, d_model), jnp.bfloat16),
            table, table,
            sub_tables, sub_tables,
            table, table,
            pltpu.VMEM((SEQ_TILE, d_mix), jnp.bfloat16),
            pltpu.VMEM((RET_HEADS, HEAD_DIM, HEAD_DIM), jnp.float32),
            pltpu.VMEM((POOL_HIST, group_w), jnp.float32),
            pltpu.SMEM((SUBS_PER_STEP,), jnp.int32),
            pltpu.SemaphoreType.DMA((N_DMA_SEMS,)),
        ],
        compiler_params=pltpu.CompilerParams(
            dimension_semantics=("arbitrary", "arbitrary"),
            vmem_limit_bytes=_vmem_limit_bytes(d_model, d_proj, d_mix, pool_rows, pool_c, batch)),
        name="hybrid_layer",
    )(x, positions, positions, w_in, w_out, pool_w, pool_scale, ret_g, pre_g, post_g)


@jax.jit
def kernel(x, positions, w_in, w_out, pool_w, pool_scale, ret_norm_g, pre_norm_g, post_norm_g):
    depth = w_in.shape[0]
    pool_w2d = pool_w.reshape(depth, -1, pool_w.shape[-1])
    for layer in range(depth):
        x = _hybrid_layer(layer, x, positions, w_in, w_out, pool_w2d,
                          pool_scale, ret_norm_g, pre_norm_g, post_norm_g)
    return x
```

```python
import functools
import math

import jax
import jax.numpy as jnp
from jax.experimental import pallas as pl
from jax.experimental.pallas import tpu as pltpu

RET_HEADS = 8
HEAD_DIM = 128
ROPE_THETA = 10000.0
POOL_WINDOWS = (2, 4, 8, 16)
N_PROJ = 6
EPS = 1e-6

SEQ_TILE = 512
SUBS_PER_STEP = 2
RET_CHUNK = 128
OUT_ROWS = 256
POOL_HIST = 16
CAST_ROWS = 16
LANES = 128
N_GAINS = 4
N_DMA_SEMS = 6 + N_GAINS


GATE_PRESCALE = 0.5


def _silu_of_half(h):
    return h + h * jnp.tanh(h)


def _cast_rows(src_ref, src_row0, src_cols, dst_ref, dst_row0, n_rows, dst_cols=slice(None), scale=None):
    def body(i, carry):
        r = pl.multiple_of(i * CAST_ROWS, CAST_ROWS)
        w = src_ref[pl.ds(src_row0 + r, CAST_ROWS), src_cols]
        if scale is not None:
            w = w * scale
        dst_ref[pl.ds(dst_row0 + r, CAST_ROWS), dst_cols] = w.astype(jnp.bfloat16)
        return carry
    jax.lax.fori_loop(0, n_rows // CAST_ROWS, body, 0)


def _stage_weights(w_in_hbm, w_out_hbm, pool_w_hbm, gains_hbm, w_in_s, w_out_s, pool_w_s, gains_s,
                   stage_ref, sems):
    n_tok = stage_ref.shape[0]
    d_model = w_out_hbm.shape[1]
    pool_c = pool_w_hbm.shape[1]
    n_out_chunks = w_out_hbm.shape[0] // n_tok
    pool_rows = pool_w_hbm.shape[0]
    n_pool_chunks = pool_rows // n_tok
    pool_col0 = n_out_chunks * d_model
    n_small = n_out_chunks + n_pool_chunks

    def small_copy(i):
        if i < n_out_chunks:
            src = w_out_hbm.at[pl.ds(i * n_tok, n_tok), :]
            dst = stage_ref.at[:, pl.ds(i * d_model, d_model)]
        elif i < n_small:
            j = i - n_out_chunks
            src = pool_w_hbm.at[pl.ds(j * n_tok, n_tok), :]
            dst = stage_ref.at[:, pl.ds(pool_col0 + j * pool_c, pool_c)]
        else:
            src, dst = gains_hbm[i - n_small], gains_s[i - n_small]
        return pltpu.make_async_copy(src, dst, sems.at[i])

    for i in range(n_small + N_GAINS):
        small_copy(i).start()
    for i in range(n_small + N_GAINS):
        small_copy(i).wait()
    for i in range(n_out_chunks):
        _cast_rows(stage_ref, 0, slice(i * d_model, (i + 1) * d_model), w_out_s, i * n_tok, n_tok)
    for j in range(n_pool_chunks):
        _cast_rows(stage_ref, 0, slice(pool_col0 + j * pool_c, pool_col0 + (j + 1) * pool_c),
                   pool_w_s, j * n_tok, n_tok)

    half = n_tok // 2
    n_in_chunks = w_in_hbm.shape[0] // half

    def in_copy(c):
        slot = c % 2
        return pltpu.make_async_copy(w_in_hbm.at[pl.ds(c * half, half), :],
                                     stage_ref.at[pl.ds(slot * half, half), :], sems.at[slot])

    group_w = w_in_hbm.shape[1] // N_PROJ
    col_ranges = [(slice(0, 3 * group_w), None), (slice(3 * group_w, 4 * group_w), GATE_PRESCALE),
                  (slice(4 * group_w, 5 * group_w), None), (slice(5 * group_w, 6 * group_w), GATE_PRESCALE)]

    in_copy(0).start()
    in_copy(1).start()
    for c in range(n_in_chunks):
        in_copy(c).wait()
        for cols, scale in col_ranges:
            _cast_rows(stage_ref, (c % 2) * half, cols, w_in_s, c * half, half, dst_cols=cols, scale=scale)
        if c + 2 < n_in_chunks:
            in_copy(c + 2).start()


def _lane_inv_freq():
    half = HEAD_DIM // 2
    lane = jax.lax.broadcasted_iota(jnp.int32, (1, HEAD_DIM), 1)
    freq_idx = (lane & (half - 1)).astype(jnp.float32)
    return jnp.exp(freq_idx * (-math.log(ROPE_THETA) / half))


def _offset_tables(cos_off_ref, sin_off_ref):
    n_tok = cos_off_ref.shape[0]
    offset = jax.lax.broadcasted_iota(jnp.int32, (n_tok, HEAD_DIM), 0).astype(jnp.float32)
    ang = offset * _lane_inv_freq()
    cos_off_ref[...] = jnp.cos(ang)
    sin_off_ref[...] = jnp.sin(ang)


def _tile_positions(pos_ref, batch_row, sub):
    return pos_ref[pl.ds(batch_row, 1), :][:, sub * SEQ_TILE:(sub + 1) * SEQ_TILE]


def _general_tables(pos, cos_ref, sin_ref):
    f32 = jnp.float32
    half = HEAD_DIM // 2
    freq_idx = jax.lax.broadcasted_iota(jnp.int32, (half, LANES), 0).astype(f32)
    inv_freq = jnp.exp(freq_idx * (-math.log(ROPE_THETA) / half))
    inv_freq = jnp.concatenate([inv_freq] * (pos.shape[1] // LANES), axis=1)
    ang = inv_freq * pos.astype(f32)
    cos_a, sin_a = jnp.cos(ang), jnp.sin(ang)
    cos_ref[...] = jnp.concatenate([cos_a, cos_a], axis=0).T
    sin_ref[...] = jnp.concatenate([-sin_a, sin_a], axis=0).T


def _is_contiguous(pos):
    offset = jax.lax.broadcasted_iota(jnp.int32, pos.shape, 1)
    return jnp.min((pos - pos[:, 0:1] == offset).astype(jnp.int32))


def _contiguous_tables(pos, cos_off_ref, sin_off_ref):
    half = HEAD_DIM // 2
    lane = jax.lax.broadcasted_iota(jnp.int32, (1, HEAD_DIM), 1)
    base = pos[:, 0:1].astype(jnp.float32) * _lane_inv_freq()
    cos_b, sin_b = jnp.cos(base), jnp.sin(base)
    sign = jnp.where(lane < half, -1.0, 1.0)
    cos_o, sin_o = cos_off_ref[...], sin_off_ref[...]
    cos_t = cos_b * cos_o - sin_b * sin_o
    sin_t = (sign * sin_b) * cos_o + (sign * cos_b) * sin_o
    return cos_t, sin_t


def _layer_kernel(layer, x_ref, pos_ref, posn_ref, w_in_all, w_out_all, pool_w_all,
                  pool_scale_hbm, ret_g_hbm, pre_g_hbm, post_g_hbm, o_ref,
                  w_in_s, w_out_s, pool_w_s, pool_scale_ref, ret_g_ref, pre_g_ref, post_g_ref,
                  proj_ref, hb_ref, cos_off_ref, sin_off_ref,
                  cos_gen_ref, sin_gen_ref, cos_ref, sin_ref, cosb_ref, sinb_ref, y_ref,
                  state_ref, uhist_ref, contig_ref, sems):
    f32, bf16 = jnp.float32, jnp.bfloat16
    w_in_hbm, w_out_hbm, pool_w_hbm = w_in_all.at[layer], w_out_all.at[layer], pool_w_all.at[layer]
    b, blk = pl.program_id(0), pl.program_id(1)
    n_blocks = pl.num_programs(1)
    n_tok = SEQ_TILE
    group_w = RET_HEADS * HEAD_DIM
    half = HEAD_DIM // 2

    @pl.when((b == 0) & (blk == 0))
    def _():
        gains_hbm = [g.at[pl.ds(layer, 1), :] for g in (pool_scale_hbm, ret_g_hbm, pre_g_hbm, post_g_hbm)]
        gains_s = [pool_scale_ref, ret_g_ref, pre_g_ref, post_g_ref]
        _stage_weights(w_in_hbm, w_out_hbm, pool_w_hbm, gains_hbm, w_in_s, w_out_s, pool_w_s, gains_s,
                       proj_ref, sems)
        _offset_tables(cos_off_ref, sin_off_ref)
        cos_gen_ref[...] = jnp.zeros_like(cos_gen_ref)
        sin_gen_ref[...] = jnp.zeros_like(sin_gen_ref)
        for sub in range(SUBS_PER_STEP):
            contig_ref[sub] = _is_contiguous(_tile_positions(pos_ref, b, sub))

    @pl.when(blk == 0)
    def _():
        state_ref[...] = jnp.zeros_like(state_ref)
        uhist_ref[...] = jnp.zeros_like(uhist_ref)

    contiguous = [contig_ref[sub] for sub in range(SUBS_PER_STEP)]
    for sub in range(SUBS_PER_STEP):
        @pl.when(contiguous[sub] == 0)
        def _():
            _general_tables(_tile_positions(pos_ref, b, sub), cos_gen_ref.at[sub], sin_gen_ref.at[sub])

    wraps = blk + 1 == n_blocks
    next_row = jnp.minimum(b + wraps.astype(jnp.int32), pl.num_programs(0) - 1)
    for sub in range(SUBS_PER_STEP):
        contig_ref[sub] = _is_contiguous(_tile_positions(posn_ref, next_row, sub))

    row = jax.lax.broadcasted_iota(jnp.int32, (RET_CHUNK, HEAD_DIM), 0)
    col = jax.lax.broadcasted_iota(jnp.int32, (RET_CHUNK, HEAD_DIM), 1)
    causal = row >= col
    row_p1 = (row + 1).astype(f32)
    k_scale = HEAD_DIM ** -0.5
    pool_c = group_w // len(POOL_WINDOWS)
    hist_row = jax.lax.broadcasted_iota(jnp.int32, (POOL_HIST, pool_c), 0)

    for sub in range(SUBS_PER_STEP):
        tok0 = sub * n_tok

        x = x_ref[tok0:tok0 + n_tok, :]
        ms = jnp.mean(x * x, axis=-1, keepdims=True)
        hb_ref[...] = (x * jax.lax.rsqrt(ms + EPS) * pre_g_ref[...]).astype(bf16)

        def project(col0, n_cols):
            cols = slice(col0, col0 + n_cols)
            proj_ref[:, cols] = jnp.dot(hb_ref[...], w_in_s[:, cols], preferred_element_type=f32)

        for seg in range(4):
            project(seg * group_w, group_w)

        use_fast = jnp.broadcast_to(contiguous[sub], (n_tok, HEAD_DIM)) != 0
        cos_fast, sin_fast = _contiguous_tables(_tile_positions(pos_ref, b, sub), cos_off_ref, sin_off_ref)
        cos_t = jnp.where(use_fast, cos_fast, cos_gen_ref[sub])
        sin_t = jnp.where(use_fast, sin_fast, sin_gen_ref[sub])
        cos_ref[...], sin_ref[...] = cos_t, sin_t
        cosb_ref[...], sinb_ref[...] = cos_t.astype(bf16), sin_t.astype(bf16)

        for hd in range(RET_HEADS):
            log_g = math.log(1.0 - 2.0 ** (-5.0 - hd))
            q_decay = jnp.exp(row_p1 * log_g).astype(bf16)
            k_decay = jnp.exp(row_p1 * (-log_g)) * k_scale
            chunk_decay = math.exp(log_g * RET_CHUNK)
            hcols = slice(hd * HEAD_DIM, (hd + 1) * HEAD_DIM)
            kcols = slice(group_w + hd * HEAD_DIM, group_w + (hd + 1) * HEAD_DIM)
            vcols = slice(2 * group_w + hd * HEAD_DIM, 2 * group_w + (hd + 1) * HEAD_DIM)
            gcols = slice(3 * group_w + hd * HEAD_DIM, 3 * group_w + (hd + 1) * HEAD_DIM)
            norm_g = ret_g_ref[:, hcols]
            for c in range(n_tok // RET_CHUNK):
                rows = slice(c * RET_CHUNK, (c + 1) * RET_CHUNK)
                cos_c, sin_c = cos_ref[rows, :], sin_ref[rows, :]
                q = proj_ref[rows, hcols]
                k = proj_ref[rows, kcols]
                vb = proj_ref[rows, vcols].astype(bf16)
                qd = ((q.astype(bf16) * cosb_ref[rows, :]
                       + pltpu.roll(q, half, axis=1).astype(bf16) * sinb_ref[rows, :]) * q_decay)
                kd = (k * cos_c + pltpu.roll(k, half, axis=1) * sin_c) * k_decay
                kdt = kd.T.astype(bf16)
                scores = jnp.dot(qd, kdt, preferred_element_type=f32)
                p = jnp.where(causal, scores, 0.0).astype(bf16)
                s_prev = state_ref[hd]
                lhs = jnp.concatenate([p, qd], axis=1)
                rhs = jnp.concatenate([vb, s_prev.astype(bf16)], axis=0)
                o = jnp.dot(lhs, rhs, preferred_element_type=f32)
                kv = jnp.dot(kdt, vb, preferred_element_type=f32)
                state_ref[hd] = (s_prev + kv) * chunk_decay
                mu = jnp.mean(o, axis=-1, keepdims=True)
                dev = o - mu
                var = jnp.mean(dev * dev, axis=-1, keepdims=True)
                gate = _silu_of_half(proj_ref[rows, gcols])
                y_ref[rows, hcols] = (dev * jax.lax.rsqrt(var + EPS) * norm_g * gate).astype(bf16)

        project(4 * group_w, 2 * group_w)

        u_off, gp_off = 4 * group_w, 5 * group_w
        tile_tok0 = (blk * SUBS_PER_STEP + sub) * n_tok
        t_head = (tile_tok0 + hist_row + 1).astype(f32)
        for g, w in enumerate(POOL_WINDOWS):
            ucols = slice(u_off + g * pool_c, u_off + (g + 1) * pool_c)
            gcols = slice(gp_off + g * pool_c, gp_off + (g + 1) * pool_c)
            pcols = slice(g * pool_c, (g + 1) * pool_c)
            u = proj_ref[:, ucols]
            s = jnp.concatenate([uhist_ref[:, pcols], u], axis=0)
            shift = 1
            while shift < w:
                s = s + pltpu.roll(s, shift, axis=0)
                shift *= 2
            wsum = s[POOL_HIST:]
            inv_head = 1.0 / jnp.minimum(t_head, float(w))
            mean = jnp.concatenate([wsum[:POOL_HIST] * inv_head, wsum[POOL_HIST:] * (1.0 / w)], axis=0)
            mixed = (mean - u).astype(bf16)
            pm = jnp.dot(mixed, pool_w_s[g * pool_c:(g + 1) * pool_c, :], preferred_element_type=f32)
            gate = _silu_of_half(proj_ref[:, gcols])
            y_ref[:, group_w + g * pool_c:group_w + (g + 1) * pool_c] = (
                pm * pool_scale_ref[:, pcols] * gate).astype(bf16)
        uhist_ref[...] = proj_ref[n_tok - POOL_HIST:, u_off:u_off + group_w]

        for r0 in range(0, n_tok, OUT_ROWS):
            out = jnp.dot(y_ref[r0:r0 + OUT_ROWS, :], w_out_s[...], preferred_element_type=f32)
            ms_o = jnp.mean(out * out, axis=-1, keepdims=True)
            orows = slice(tok0 + r0, tok0 + r0 + OUT_ROWS)
            o_ref[orows, :] = x_ref[orows, :] + out * jax.lax.rsqrt(ms_o + EPS) * post_g_ref[...]


def _vmem_limit_bytes(d_model, d_proj, d_mix, pool_rows, pool_c, batch):
    step_tok = SEQ_TILE * SUBS_PER_STEP
    scratch = (2 * (d_model * d_proj + d_mix * d_model + pool_rows * pool_c)
               + 4 * SEQ_TILE * d_proj
               + 2 * SEQ_TILE * (d_mix + d_model)
               + (4 + 2 * SUBS_PER_STEP) * 4 * SEQ_TILE * HEAD_DIM
               + 4 * RET_HEADS * HEAD_DIM * HEAD_DIM + 4 * POOL_HIST * (d_mix // 2))
    blocks = 2 * (2 * 4 * step_tok * d_model + 2 * 4 * max(batch, 8) * step_tok)
    temporaries = 3 * SEQ_TILE * d_proj
    return scratch + blocks + temporaries


def _hybrid_layer(layer, x, positions, w_in, w_out, pool_w, pool_scale, ret_g, pre_g, post_g):
    batch, seq, d_model = x.shape
    d_proj = w_in.shape[2]
    d_mix = w_out.shape[1]
    group_w = d_mix // 2
    _, pool_rows, pool_c = pool_w.shape
    step_tok = SEQ_TILE * SUBS_PER_STEP
    n_blocks = seq // step_tok
    assert seq % step_tok == 0 and SEQ_TILE % RET_CHUNK == 0 and SEQ_TILE % (2 * CAST_ROWS) == 0
    assert SEQ_TILE % OUT_ROWS == 0 and SEQ_TILE % LANES == 0
    assert group_w == RET_HEADS * HEAD_DIM and d_proj == N_PROJ * group_w
    assert d_mix % SEQ_TILE == 0 and pool_rows % SEQ_TILE == 0 and d_model % (SEQ_TILE // 2) == 0
    n_small = d_mix // SEQ_TILE + pool_rows // SEQ_TILE
    assert (d_mix // SEQ_TILE) * d_model + (pool_rows // SEQ_TILE) * pool_c <= d_proj
    assert n_small + N_GAINS <= N_DMA_SEMS

    def pos_next_map(b, t):
        step = jnp.minimum(b * n_blocks + t + 1, batch * n_blocks - 1)
        return 0, step % n_blocks

    hbm = pl.BlockSpec(memory_space=pltpu.HBM)
    table = pltpu.VMEM((SEQ_TILE, HEAD_DIM), jnp.float32)
    sub_tables = pltpu.VMEM((SUBS_PER_STEP, SEQ_TILE, HEAD_DIM), jnp.float32)
    return pl.pallas_call(
        functools.partial(_layer_kernel, layer),
        grid=(batch, n_blocks),
        in_specs=[
            pl.BlockSpec((None, step_tok, d_model), lambda b, t: (b, t, 0)),
            pl.BlockSpec((batch, step_tok), lambda b, t: (0, t)),
            pl.BlockSpec((batch, step_tok), pos_next_map),
            hbm, hbm, hbm,
            hbm, hbm, hbm, hbm,
        ],
        out_specs=pl.BlockSpec((None, step_tok, d_model), lambda b, t: (b, t, 0)),
        out_shape=jax.ShapeDtypeStruct(x.shape, x.dtype),
        scratch_shapes=[
            pltpu.VMEM((d_model, d_proj), jnp.bfloat16),
            pltpu.VMEM((d_mix, d_model), jnp.bfloat16),
            pltpu.VMEM((pool_rows, pool_c), jnp.bfloat16),
            pltpu.VMEM((1, group_w), jnp.float32),
            pltpu.VMEM((1, group_w), jnp.float32),
            pltpu.VMEM((1, d_model), jnp.float32),
            pltpu.VMEM((1, d_model), jnp.float32),
            pltpu.VMEM((SEQ_TILE, d_proj), jnp.float32),
            pltpu.VMEM((SEQ_TILE, d_model), jnp.bfloat16),
            table, table,
            sub_tables, sub_tables,
            table, table,
            pltpu.VMEM((SEQ_TILE, HEAD_DIM), jnp.bfloat16),
            pltpu.VMEM((SEQ_TILE, HEAD_DIM), jnp.bfloat16),
            pltpu.VMEM((SEQ_TILE, d_mix), jnp.bfloat16),
            pltpu.VMEM((RET_HEADS, HEAD_DIM, HEAD_DIM), jnp.float32),
            pltpu.VMEM((POOL_HIST, group_w), jnp.float32),
            pltpu.SMEM((SUBS_PER_STEP,), jnp.int32),
            pltpu.SemaphoreType.DMA((N_DMA_SEMS,)),
        ],
        compiler_params=pltpu.CompilerParams(
            dimension_semantics=("arbitrary", "arbitrary"),
            vmem_limit_bytes=_vmem_limit_bytes(d_model, d_proj, d_mix, pool_rows, pool_c, batch)),
        name="hybrid_layer",
    )(x, positions, positions, w_in, w_out, pool_w, pool_scale, ret_g, pre_g, post_g)


@jax.jit
def kernel(x, positions, w_in, w_out, pool_w, pool_scale, ret_norm_g, pre_norm_g, post_norm_g):
    depth = w_in.shape[0]
    pool_w2d = pool_w.reshape(depth, -1, pool_w.shape[-1])
    for layer in range(depth):
        x = _hybrid_layer(layer, x, positions, w_in, w_out, pool_w2d,
                          pool_scale, ret_norm_g, pre_norm_g, post_norm_g)
    return x
```

```python
import functools
import math

import jax
import jax.numpy as jnp
from jax.experimental import pallas as pl
from jax.experimental.pallas import tpu as pltpu

RET_HEADS = 8
HEAD_DIM = 128
ROPE_THETA = 10000.0
POOL_WINDOWS = (2, 4, 8, 16)
N_PROJ = 6
EPS = 1e-6

SEQ_TILE = 512
SUBS_PER_STEP = 2
RET_CHUNK = 128
OUT_ROWS = 256
POOL_HIST = 16
CAST_ROWS = 16
LANES = 128
N_GAINS = 4
N_DMA_SEMS = 6 + N_GAINS


GATE_PRESCALE = 0.5


def _silu_of_half(h):
    return h + h * jnp.tanh(h)


def _cast_rows(src_ref, src_row0, src_cols, dst_ref, dst_row0, n_rows, dst_cols=slice(None),
               scale=None, row_scale=None, col_scale=None):
    def body(i, carry):
        r = pl.multiple_of(i * CAST_ROWS, CAST_ROWS)
        w = src_ref[pl.ds(src_row0 + r, CAST_ROWS), src_cols]
        if scale is not None:
            w = w * scale
        if row_scale is not None:
            rs_ref, rs_row0 = row_scale
            rs = rs_ref[pl.ds(rs_row0 + r, CAST_ROWS), :]
            w = w * jnp.concatenate([rs] * (w.shape[1] // LANES), axis=1)
        if col_scale is not None:
            cs_ref, cs_cols = col_scale
            w = w * cs_ref[:, cs_cols]
        dst_ref[pl.ds(dst_row0 + r, CAST_ROWS), dst_cols] = w.astype(jnp.bfloat16)
        return carry
    jax.lax.fori_loop(0, n_rows // CAST_ROWS, body, 0)


def _stage_weights(w_in_hbm, w_out_hbm, pool_w_hbm, gains_hbm, w_in_s, w_out_s, pool_w_s, gains_s,
                   row_scale_ref, stage_ref, sems):
    pool_scale_ref, ret_g_ref, pre_g_ref, _ = gains_s
    n_tok = stage_ref.shape[0]
    d_model = w_out_hbm.shape[1]
    group_w = w_in_hbm.shape[1] // N_PROJ
    pool_c = pool_w_hbm.shape[1]
    n_out_chunks = w_out_hbm.shape[0] // n_tok
    pool_rows = pool_w_hbm.shape[0]
    n_pool_chunks = pool_rows // n_tok
    pool_col0 = n_out_chunks * d_model
    n_small = n_out_chunks + n_pool_chunks

    def small_copy(i):
        if i < n_out_chunks:
            src = w_out_hbm.at[pl.ds(i * n_tok, n_tok), :]
            dst = stage_ref.at[:, pl.ds(i * d_model, d_model)]
        elif i < n_small:
            j = i - n_out_chunks
            src = pool_w_hbm.at[pl.ds(j * n_tok, n_tok), :]
            dst = stage_ref.at[:, pl.ds(pool_col0 + j * pool_c, pool_c)]
        else:
            src, dst = gains_hbm[i - n_small], gains_s[i - n_small]
        return pltpu.make_async_copy(src, dst, sems.at[i])

    for i in range(n_small + N_GAINS):
        small_copy(i).start()
    for i in range(n_small + N_GAINS):
        small_copy(i).wait()

    row_scale_ref[0] = jnp.broadcast_to(pre_g_ref[...], (LANES, d_model)).T
    row_scale_ref[1] = jnp.broadcast_to(ret_g_ref[...], (LANES, group_w)).T

    for i in range(n_out_chunks):
        in_ret_half = (i + 1) * n_tok <= group_w
        _cast_rows(stage_ref, 0, slice(i * d_model, (i + 1) * d_model), w_out_s, i * n_tok, n_tok,
                   row_scale=(row_scale_ref.at[1], i * n_tok) if in_ret_half else None)
    for g in range(pool_rows // pool_c):
        j, r0 = divmod(g * pool_c, n_tok)
        _cast_rows(stage_ref, r0, slice(pool_col0 + j * pool_c, pool_col0 + (j + 1) * pool_c),
                   pool_w_s, g * pool_c, pool_c,
                   col_scale=(pool_scale_ref, slice(g * pool_c, (g + 1) * pool_c)))

    half = n_tok // 2
    n_in_chunks = w_in_hbm.shape[0] // half

    def in_copy(c):
        slot = c % 2
        return pltpu.make_async_copy(w_in_hbm.at[pl.ds(c * half, half), :],
                                     stage_ref.at[pl.ds(slot * half, half), :], sems.at[slot])

    col_ranges = [(slice(0, 3 * group_w), None), (slice(3 * group_w, 4 * group_w), GATE_PRESCALE),
                  (slice(4 * group_w, 5 * group_w), None), (slice(5 * group_w, 6 * group_w), GATE_PRESCALE)]

    in_copy(0).start()
    in_copy(1).start()
    for c in range(n_in_chunks):
        in_copy(c).wait()
        for cols, scale in col_ranges:
            _cast_rows(stage_ref, (c % 2) * half, cols, w_in_s, c * half, half, dst_cols=cols,
                       scale=scale, row_scale=(row_scale_ref.at[0], c * half))
        if c + 2 < n_in_chunks:
            in_copy(c + 2).start()


def _lane_inv_freq():
    half = HEAD_DIM // 2
    lane = jax.lax.broadcasted_iota(jnp.int32, (1, HEAD_DIM), 1)
    freq_idx = (lane & (half - 1)).astype(jnp.float32)
    return jnp.exp(freq_idx * (-math.log(ROPE_THETA) / half))


def _offset_tables(cos_off_ref, sin_off_ref):
    n_tok = cos_off_ref.shape[0]
    offset = jax.lax.broadcasted_iota(jnp.int32, (n_tok, HEAD_DIM), 0).astype(jnp.float32)
    ang = offset * _lane_inv_freq()
    cos_off_ref[...] = jnp.cos(ang)
    sin_off_ref[...] = jnp.sin(ang)


def _tile_positions(pos_ref, batch_row, sub):
    return pos_ref[pl.ds(batch_row, 1), :][:, sub * SEQ_TILE:(sub + 1) * SEQ_TILE]


def _general_tables(pos, cos_ref, sin_ref):
    f32 = jnp.float32
    half = HEAD_DIM // 2
    freq_idx = jax.lax.broadcasted_iota(jnp.int32, (half, LANES), 0).astype(f32)
    inv_freq = jnp.exp(freq_idx * (-math.log(ROPE_THETA) / half))
    inv_freq = jnp.concatenate([inv_freq] * (pos.shape[1] // LANES), axis=1)
    ang = inv_freq * pos.astype(f32)
    cos_a, sin_a = jnp.cos(ang), jnp.sin(ang)
    cos_ref[...] = jnp.concatenate([cos_a, cos_a], axis=0).T
    sin_ref[...] = jnp.concatenate([-sin_a, sin_a], axis=0).T


def _is_contiguous(pos):
    offset = jax.lax.broadcasted_iota(jnp.int32, pos.shape, 1)
    return jnp.min((pos - pos[:, 0:1] == offset).astype(jnp.int32))


def _contiguous_tables(pos, cos_off_ref, sin_off_ref):
    half = HEAD_DIM // 2
    lane = jax.lax.broadcasted_iota(jnp.int32, (1, HEAD_DIM), 1)
    base = pos[:, 0:1].astype(jnp.float32) * _lane_inv_freq()
    cos_b, sin_b = jnp.cos(base), jnp.sin(base)
    sign = jnp.where(lane < half, -1.0, 1.0)
    cos_o, sin_o = cos_off_ref[...], sin_off_ref[...]
    cos_t = cos_b * cos_o - sin_b * sin_o
    sin_t = (sign * sin_b) * cos_o + (sign * cos_b) * sin_o
    return cos_t, sin_t


def _layer_kernel(layer, x_ref, pos_ref, posn_ref, w_in_all, w_out_all, pool_w_all,
                  pool_scale_hbm, ret_g_hbm, pre_g_hbm, post_g_hbm, o_ref,
                  w_in_s, w_out_s, pool_w_s, pool_scale_ref, ret_g_ref, pre_g_ref, post_g_ref,
                  proj_ref, hb_ref, cos_off_ref, sin_off_ref,
                  cos_gen_ref, sin_gen_ref, cos_ref, sin_ref, cosb_ref, sinb_ref, y_ref,
                  state_ref, uhist_ref, row_scale_ref, contig_ref, sems):
    f32, bf16 = jnp.float32, jnp.bfloat16
    w_in_hbm, w_out_hbm, pool_w_hbm = w_in_all.at[layer], w_out_all.at[layer], pool_w_all.at[layer]
    b, blk = pl.program_id(0), pl.program_id(1)
    n_blocks = pl.num_programs(1)
    n_tok = SEQ_TILE
    group_w = RET_HEADS * HEAD_DIM
    half = HEAD_DIM // 2

    @pl.when((b == 0) & (blk == 0))
    def _():
        gains_hbm = [g.at[pl.ds(layer, 1), :] for g in (pool_scale_hbm, ret_g_hbm, pre_g_hbm, post_g_hbm)]
        gains_s = [pool_scale_ref, ret_g_ref, pre_g_ref, post_g_ref]
        _stage_weights(w_in_hbm, w_out_hbm, pool_w_hbm, gains_hbm, w_in_s, w_out_s, pool_w_s, gains_s,
                       row_scale_ref, proj_ref, sems)
        _offset_tables(cos_off_ref, sin_off_ref)
        cos_gen_ref[...] = jnp.zeros_like(cos_gen_ref)
        sin_gen_ref[...] = jnp.zeros_like(sin_gen_ref)
        for sub in range(SUBS_PER_STEP):
            contig_ref[sub] = _is_contiguous(_tile_positions(pos_ref, b, sub))

    @pl.when(blk == 0)
    def _():
        state_ref[...] = jnp.zeros_like(state_ref)
        uhist_ref[...] = jnp.zeros_like(uhist_ref)

    contiguous = [contig_ref[sub] for sub in range(SUBS_PER_STEP)]
    for sub in range(SUBS_PER_STEP):
        @pl.when(contiguous[sub] == 0)
        def _():
            _general_tables(_tile_positions(pos_ref, b, sub), cos_gen_ref.at[sub], sin_gen_ref.at[sub])

    wraps = blk + 1 == n_blocks
    next_row = jnp.minimum(b + wraps.astype(jnp.int32), pl.num_programs(0) - 1)
    for sub in range(SUBS_PER_STEP):
        contig_ref[sub] = _is_contiguous(_tile_positions(posn_ref, next_row, sub))

    row = jax.lax.broadcasted_iota(jnp.int32, (RET_CHUNK, HEAD_DIM), 0)
    col = jax.lax.broadcasted_iota(jnp.int32, (RET_CHUNK, HEAD_DIM), 1)
    causal = row >= col
    row_p1 = (row + 1).astype(f32)
    k_scale = HEAD_DIM ** -0.5
    pool_c = group_w // len(POOL_WINDOWS)
    hist_row = jax.lax.broadcasted_iota(jnp.int32, (POOL_HIST, pool_c), 0)

    for sub in range(SUBS_PER_STEP):
        tok0 = sub * n_tok

        x = x_ref[tok0:tok0 + n_tok, :]
        ms = jnp.mean(x * x, axis=-1, keepdims=True)
        hb_ref[...] = (x * jax.lax.rsqrt(ms + EPS)).astype(bf16)

        def project(col0, n_cols):
            cols = slice(col0, col0 + n_cols)
            proj_ref[:, cols] = jnp.dot(hb_ref[...], w_in_s[:, cols], preferred_element_type=f32)

        for seg in range(4):
            project(seg * group_w, group_w)

        use_fast = jnp.broadcast_to(contiguous[sub], (n_tok, HEAD_DIM)) != 0
        cos_fast, sin_fast = _contiguous_tables(_tile_positions(pos_ref, b, sub), cos_off_ref, sin_off_ref)
        cos_t = jnp.where(use_fast, cos_fast, cos_gen_ref[sub])
        sin_t = jnp.where(use_fast, sin_fast, sin_gen_ref[sub])
        cos_ref[...], sin_ref[...] = cos_t, sin_t
        cosb_ref[...], sinb_ref[...] = cos_t.astype(bf16), sin_t.astype(bf16)

        for hd in range(RET_HEADS):
            log_g = math.log(1.0 - 2.0 ** (-5.0 - hd))
            q_decay = jnp.exp(row_p1 * log_g).astype(bf16)
            k_decay = jnp.exp(row_p1 * (-log_g)) * k_scale
            chunk_decay = math.exp(log_g * RET_CHUNK)
            hcols = slice(hd * HEAD_DIM, (hd + 1) * HEAD_DIM)
            kcols = slice(group_w + hd * HEAD_DIM, group_w + (hd + 1) * HEAD_DIM)
            vcols = slice(2 * group_w + hd * HEAD_DIM, 2 * group_w + (hd + 1) * HEAD_DIM)
            gcols = slice(3 * group_w + hd * HEAD_DIM, 3 * group_w + (hd + 1) * HEAD_DIM)
            for c in range(n_tok // RET_CHUNK):
                rows = slice(c * RET_CHUNK, (c + 1) * RET_CHUNK)
                cos_c, sin_c = cos_ref[rows, :], sin_ref[rows, :]
                q = proj_ref[rows, hcols]
                k = proj_ref[rows, kcols]
                vb = proj_ref[rows, vcols].astype(bf16)
                qd = ((q.astype(bf16) * cosb_ref[rows, :]
                       + pltpu.roll(q, half, axis=1).astype(bf16) * sinb_ref[rows, :]) * q_decay)
                kd = (k * cos_c + pltpu.roll(k, half, axis=1) * sin_c) * k_decay
                kdt = kd.T.astype(bf16)
                scores = jnp.dot(qd, kdt, preferred_element_type=f32)
                p = jnp.where(causal, scores, 0.0).astype(bf16)
                s_prev = state_ref[hd]
                lhs = jnp.concatenate([p, qd], axis=1)
                rhs = jnp.concatenate([vb, s_prev.astype(bf16)], axis=0)
                o = jnp.dot(lhs, rhs, preferred_element_type=f32)
                kv = jnp.dot(kdt, vb, preferred_element_type=f32)
                state_ref[hd] = (s_prev + kv) * chunk_decay
                mu = jnp.mean(o, axis=-1, keepdims=True)
                dev = o - mu
                var = jnp.mean(dev * dev, axis=-1, keepdims=True)
                gate = _silu_of_half(proj_ref[rows, gcols])
                y_ref[rows, hcols] = (dev * jax.lax.rsqrt(var + EPS) * gate).astype(bf16)

        project(4 * group_w, 2 * group_w)

        u_off, gp_off = 4 * group_w, 5 * group_w
        tile_tok0 = (blk * SUBS_PER_STEP + sub) * n_tok
        t_head = (tile_tok0 + hist_row + 1).astype(f32)
        for g, w in enumerate(POOL_WINDOWS):
            ucols = slice(u_off + g * pool_c, u_off + (g + 1) * pool_c)
            gcols = slice(gp_off + g * pool_c, gp_off + (g + 1) * pool_c)
            pcols = slice(g * pool_c, (g + 1) * pool_c)
            u = proj_ref[:, ucols]
            s = jnp.concatenate([uhist_ref[:, pcols], u], axis=0)
            shift = 1
            while shift < w:
                s = s + pltpu.roll(s, shift, axis=0)
                shift *= 2
            wsum = s[POOL_HIST:]
            inv_head = 1.0 / jnp.minimum(t_head, float(w))
            mean = jnp.concatenate([wsum[:POOL_HIST] * inv_head, wsum[POOL_HIST:] * (1.0 / w)], axis=0)
            mixed = (mean - u).astype(bf16)
            pm = jnp.dot(mixed, pool_w_s[g * pool_c:(g + 1) * pool_c, :], preferred_element_type=f32)
            gate = _silu_of_half(proj_ref[:, gcols])
            y_ref[:, group_w + g * pool_c:group_w + (g + 1) * pool_c] = (pm * gate).astype(bf16)
        uhist_ref[...] = proj_ref[n_tok - POOL_HIST:, u_off:u_off + group_w]

        for r0 in range(0, n_tok, OUT_ROWS):
            out = jnp.dot(y_ref[r0:r0 + OUT_ROWS, :], w_out_s[...], preferred_element_type=f32)
            ms_o = jnp.mean(out * out, axis=-1, keepdims=True)
            orows = slice(tok0 + r0, tok0 + r0 + OUT_ROWS)
            o_ref[orows, :] = x_ref[orows, :] + out * jax.lax.rsqrt(ms_o + EPS) * post_g_ref[...]


def _vmem_limit_bytes(d_model, d_proj, d_mix, pool_rows, pool_c, batch):
    step_tok = SEQ_TILE * SUBS_PER_STEP
    scratch = (2 * (d_model * d_proj + d_mix * d_model + pool_rows * pool_c)
               + 4 * SEQ_TILE * d_proj
               + 2 * SEQ_TILE * (d_mix + d_model)
               + (4 + 2 * SUBS_PER_STEP) * 4 * SEQ_TILE * HEAD_DIM
               + 4 * RET_HEADS * HEAD_DIM * HEAD_DIM + 4 * POOL_HIST * (d_mix // 2)
               + 2 * 4 * max(d_model, d_mix // 2) * LANES)
    blocks = 2 * (2 * 4 * step_tok * d_model + 2 * 4 * max(batch, 8) * step_tok)
    temporaries = 3 * SEQ_TILE * d_proj
    return scratch + blocks + temporaries


def _hybrid_layer(layer, x, positions, w_in, w_out, pool_w, pool_scale, ret_g, pre_g, post_g):
    batch, seq, d_model = x.shape
    d_proj = w_in.shape[2]
    d_mix = w_out.shape[1]
    group_w = d_mix // 2
    _, pool_rows, pool_c = pool_w.shape
    step_tok = SEQ_TILE * SUBS_PER_STEP
    n_blocks = seq // step_tok
    assert seq % step_tok == 0 and SEQ_TILE % RET_CHUNK == 0 and SEQ_TILE % (2 * CAST_ROWS) == 0
    assert SEQ_TILE % OUT_ROWS == 0 and SEQ_TILE % LANES == 0
    assert group_w == RET_HEADS * HEAD_DIM and d_proj == N_PROJ * group_w
    assert d_mix % SEQ_TILE == 0 and pool_rows % SEQ_TILE == 0 and d_model % (SEQ_TILE // 2) == 0
    n_small = d_mix // SEQ_TILE + pool_rows // SEQ_TILE
    assert (d_mix // SEQ_TILE) * d_model + (pool_rows // SEQ_TILE) * pool_c <= d_proj
    assert n_small + N_GAINS <= N_DMA_SEMS

    def pos_next_map(b, t):
        step = jnp.minimum(b * n_blocks + t + 1, batch * n_blocks - 1)
        return 0, step % n_blocks

    hbm = pl.BlockSpec(memory_space=pltpu.HBM)
    table = pltpu.VMEM((SEQ_TILE, HEAD_DIM), jnp.float32)
    sub_tables = pltpu.VMEM((SUBS_PER_STEP, SEQ_TILE, HEAD_DIM), jnp.float32)
    return pl.pallas_call(
        functools.partial(_layer_kernel, layer),
        grid=(batch, n_blocks),
        in_specs=[
            pl.BlockSpec((None, step_tok, d_model), lambda b, t: (b, t, 0)),
            pl.BlockSpec((batch, step_tok), lambda b, t: (0, t)),
            pl.BlockSpec((batch, step_tok), pos_next_map),
            hbm, hbm, hbm,
            hbm, hbm, hbm, hbm,
        ],
        out_specs=pl.BlockSpec((None, step_tok, d_model), lambda b, t: (b, t, 0)),
        out_shape=jax.ShapeDtypeStruct(x.shape, x.dtype),
        scratch_shapes=[
            pltpu.VMEM((d_model, d_proj), jnp.bfloat16),
            pltpu.VMEM((d_mix, d_model), jnp.bfloat16),
            pltpu.VMEM((pool_rows, pool_c), jnp.bfloat16),
            pltpu.VMEM((1, group_w), jnp.float32),
            pltpu.VMEM((1, group_w), jnp.float32),
            pltpu.VMEM((1, d_model), jnp.float32),
            pltpu.VMEM((1, d_model), jnp.float32),
            pltpu.VMEM((SEQ_TILE, d_proj), jnp.float32),
            pltpu.VMEM((SEQ_TILE, d_model), jnp.bfloat16),
            table, table,
            sub_tables, sub_tables,
            table, table,
            pltpu.VMEM((SEQ_TILE, HEAD_DIM), jnp.bfloat16),
            pltpu.VMEM((SEQ_TILE, HEAD_DIM), jnp.bfloat16),
            pltpu.VMEM((SEQ_TILE, d_mix), jnp.bfloat16),
            pltpu.VMEM((RET_HEADS, HEAD_DIM, HEAD_DIM), jnp.float32),
            pltpu.VMEM((POOL_HIST, group_w), jnp.float32),
            pltpu.VMEM((2, max(d_model, group_w), LANES), jnp.float32),
            pltpu.SMEM((SUBS_PER_STEP,), jnp.int32),
            pltpu.SemaphoreType.DMA((N_DMA_SEMS,)),
        ],
        compiler_params=pltpu.CompilerParams(
            dimension_semantics=("arbitrary", "arbitrary"),
            vmem_limit_bytes=_vmem_limit_bytes(d_model, d_proj, d_mix, pool_rows, pool_c, batch)),
        name="hybrid_layer",
    )(x, positions, positions, w_in, w_out, pool_w, pool_scale, ret_g, pre_g, post_g)


@jax.jit
def kernel(x, positions, w_in, w_out, pool_w, pool_scale, ret_norm_g, pre_norm_g, post_norm_g):
    depth = w_in.shape[0]
    pool_w2d = pool_w.reshape(depth, -1, pool_w.shape[-1])
    for layer in range(depth):
        x = _hybrid_layer(layer, x, positions, w_in, w_out, pool_w2d,
                          pool_scale, ret_norm_g, pre_norm_g, post_norm_g)
    return x
```
